```python
import jax, jax.numpy as jnp
from jax import lax
import numpy as np

D_MODEL = 2048
BATCH = 16
SEQ = 2048
DEPTH = 1
DEC_BATCH = 32
DEC_SEQ = 8
PAST_LEN = 16384
PAGE_SIZE = 128

MIX_DIM = D_MODEL
ATT_HEADS = 16
HEAD_DIM = 64
ATT_DIM = ATT_HEADS * HEAD_DIM
CONV_DIM = MIX_DIM - ATT_DIM
CONV_GROUPS = 16
CONV_W = 3
D_FF = 4 * D_MODEL
Q_BLOCK = 128
EPS = 1e-6
FORGET_BIAS_INIT = 2.0
N_IN = 3 * ATT_DIM + ATT_HEADS + 3 * CONV_DIM

kernel_name = 'fox_shortconv_hymba_decode_step'

F32 = jnp.float32


def rmsnorm(x, g):
    xf = x.astype(F32)
    y = xf * lax.rsqrt(jnp.mean(xf * xf, axis=-1, keepdims=True) + EPS)
    return (y * g.astype(F32)).astype(x.dtype)


def project(x, norm_mix, w_in, b_f):
    b, s, _ = x.shape
    h = rmsnorm(x, norm_mix)
    p = jnp.einsum('bsd,dn->bsn', h, w_in)
    cuts = [ATT_DIM, 2 * ATT_DIM, 3 * ATT_DIM, 3 * ATT_DIM + ATT_HEADS,
            3 * ATT_DIM + ATT_HEADS + CONV_DIM, 3 * ATT_DIM + ATT_HEADS + 2 * CONV_DIM]
    q, k, v, fl, xc, bg, cg = jnp.split(p, cuts, axis=-1)
    q = q.reshape(b, s, ATT_HEADS, HEAD_DIM)
    k = k.reshape(b, s, ATT_HEADS, HEAD_DIM)
    v = v.reshape(b, s, ATT_HEADS, HEAD_DIM)
    logf = jax.nn.log_sigmoid(fl.astype(F32) + b_f.astype(F32))
    u = cg * xc
    return q, k, v, logf, u, bg


def fox_attention_prompt(q, k, v, logf):
    b, s, h, d = q.shape
    nb = s // Q_BLOCK
    scale = d ** -0.5
    c = jnp.cumsum(logf, axis=1).transpose(0, 2, 1)
    qb = q.reshape(b, nb, Q_BLOCK, h, d).transpose(1, 0, 2, 3, 4)
    cb = c.reshape(b, h, nb, Q_BLOCK).transpose(2, 0, 1, 3)
    kpos = jnp.arange(s)

    def block(args):
        i, qi, ci = args
        sc = jnp.einsum('bqhd,bkhd->bhqk', qi, k, preferred_element_type=F32) * scale
        bias = ci[..., :, None] - c[..., None, :]
        qpos = i * Q_BLOCK + jnp.arange(Q_BLOCK)
        mask = kpos[None, :] <= qpos[:, None]
        pr = jax.nn.softmax(jnp.where(mask, sc + bias, -jnp.inf), axis=-1)
        return jnp.einsum('bhqk,bkhd->bqhd', pr.astype(v.dtype), v)

    out = lax.map(block, (jnp.arange(nb), qb, cb))
    return out.transpose(1, 0, 2, 3, 4).reshape(b, s, h * d)


def fox_attention_sample(q, k_new, v_new, logf_new, cache_k, cache_v, cache_logf, page_table):
    bd, t, h, d = q.shape
    scale = d ** -0.5
    k_past = cache_k[page_table].reshape(bd, -1, h, d)
    v_past = cache_v[page_table].reshape(bd, -1, h, d)
    lf_past = cache_logf[page_table].reshape(bd, -1, h).astype(F32)
    p_len = k_past.shape[1]
    c = jnp.cumsum(jnp.concatenate([lf_past, logf_new], axis=1), axis=1).transpose(0, 2, 1)
    c_past, c_new = c[..., :p_len], c[..., p_len:]
    s_past = jnp.einsum('bqhd,bkhd->bhqk', q, k_past, preferred_element_type=F32) * scale
    s_past = s_past + c_new[..., :, None] - c_past[..., None, :]
    s_new = jnp.einsum('bqhd,bkhd->bhqk', q, k_new, preferred_element_type=F32) * scale
    s_new = s_new + c_new[..., :, None] - c_new[..., None, :]
    causal = jnp.arange(t)[None, :] <= jnp.arange(t)[:, None]
    s_new = jnp.where(causal, s_new, -jnp.inf)
    pr = jax.nn.softmax(jnp.concatenate([s_past, s_new], axis=-1), axis=-1)
    out = (jnp.einsum('bhqk,bkhd->bqhd', pr[..., :p_len].astype(v_past.dtype), v_past)
           + jnp.einsum('bhqk,bkhd->bqhd', pr[..., p_len:].astype(v_new.dtype), v_new))
    return out.reshape(bd, t, h * d)


def short_conv(u, prev, conv_w):
    s = u.shape[1]
    u_pad = jnp.concatenate([prev.astype(u.dtype), u], axis=1)
    y = u_pad[:, 0:s] * conv_w[0]
    for i in range(1, CONV_W):
        y = y + u_pad[:, i:i + s] * conv_w[i]
    return y, u_pad[:, -(CONV_W - 1):]


def merge_and_mlp(x, y_att, y_conv, bg, norm_att_out, norm_conv_out, w_out, norm_mlp, w_up, w_down):
    y_c = bg * y_conv
    m = jnp.concatenate([rmsnorm(y_att, norm_att_out), rmsnorm(y_c, norm_conv_out)], axis=-1)
    x = x + jnp.einsum('bsm,md->bsd', m, w_out)
    h = rmsnorm(x, norm_mlp)
    z = jnp.square(jax.nn.relu(jnp.einsum('bsd,df->bsf', h, w_up)))
    return x + jnp.einsum('bsf,fd->bsd', z, w_down)


def setup_inputs(seed: int = 0) -> dict:
    key = jax.random.key(seed)
    ks = jax.random.split(key, 20)
    n_pages = PAST_LEN // PAGE_SIZE
    n_used = DEC_BATCH * n_pages
    n_pool = (n_used * 5) // 4
    nrm = jax.random.normal
    x_prompt = nrm(ks[0], (BATCH, SEQ, D_MODEL), F32)
    x_sample = nrm(ks[1], (DEC_BATCH, DEC_SEQ, D_MODEL), F32)
    cache_k = nrm(ks[2], (DEPTH, n_pool, PAGE_SIZE, ATT_HEADS, HEAD_DIM), F32)
    cache_v = nrm(ks[3], (DEPTH, n_pool, PAGE_SIZE, ATT_HEADS, HEAD_DIM), F32)
    cache_logf = jax.nn.log_sigmoid(FORGET_BIAS_INIT + nrm(ks[4], (DEPTH, n_pool, PAGE_SIZE, ATT_HEADS), F32))
    state_conv = nrm(ks[5], (DEPTH, DEC_BATCH, CONV_W - 1, CONV_DIM), F32)
    page_table = jax.random.permutation(ks[6], n_pool)[:n_used].reshape(DEC_BATCH, n_pages).astype(jnp.int32)
    norm_mix = 1.0 + 0.05 * nrm(ks[7], (DEPTH, D_MODEL), F32)
    w_in = nrm(ks[8], (DEPTH, D_MODEL, N_IN), F32) * D_MODEL ** -0.5
    b_f = FORGET_BIAS_INIT + 0.1 * nrm(ks[9], (DEPTH, ATT_HEADS), F32)
    conv_w = nrm(ks[10], (DEPTH, CONV_W, CONV_DIM), F32) * CONV_W ** -0.5
    norm_att_out = 1.0 + 0.05 * nrm(ks[11], (DEPTH, ATT_DIM), F32)
    norm_conv_out = 1.0 + 0.05 * nrm(ks[12], (DEPTH, CONV_DIM), F32)
    w_out = nrm(ks[13], (DEPTH, MIX_DIM, D_MODEL), F32) * MIX_DIM ** -0.5
    norm_mlp = 1.0 + 0.05 * nrm(ks[14], (DEPTH, D_MODEL), F32)
    w_up = nrm(ks[15], (DEPTH, D_MODEL, D_FF), F32) * D_MODEL ** -0.5
    w_down = nrm(ks[16], (DEPTH, D_FF, D_MODEL), F32) * D_FF ** -0.5
    norm_final = 1.0 + 0.05 * nrm(ks[17], (D_MODEL,), F32)
    return {'x_prompt': x_prompt, 'x_sample': x_sample, 'cache_k': cache_k, 'cache_v': cache_v,
            'cache_logf': cache_logf, 'state_conv': state_conv, 'page_table': page_table,
            'norm_mix': norm_mix, 'w_in': w_in, 'b_f': b_f, 'conv_w': conv_w,
            'norm_att_out': norm_att_out, 'norm_conv_out': norm_conv_out, 'w_out': w_out,
            'norm_mlp': norm_mlp, 'w_up': w_up, 'w_down': w_down, 'norm_final': norm_final}


def reference(x_prompt, x_sample, cache_k, cache_v, cache_logf, state_conv, page_table,
              norm_mix, w_in, b_f, conv_w, norm_att_out, norm_conv_out, w_out,
              norm_mlp, w_up, w_down, norm_final):
    xp, xs = x_prompt, x_sample
    kp_l, vp_l, lfp_l, cp_l = [], [], [], []
    ks_l, vs_l, lfs_l, cs_l = [], [], [], []
    for l in range(DEPTH):
        q, k, v, lf, u, bg = project(xp, norm_mix[l], w_in[l], b_f[l])
        y_att = fox_attention_prompt(q, k, v, lf)
        prev0 = jnp.zeros((xp.shape[0], CONV_W - 1, CONV_DIM), u.dtype)
        y_conv, st = short_conv(u, prev0, conv_w[l])
        xp = merge_and_mlp(xp, y_att, y_conv, bg, norm_att_out[l], norm_conv_out[l], w_out[l],
                           norm_mlp[l], w_up[l], w_down[l])
        kp_l.append(k); vp_l.append(v); lfp_l.append(lf); cp_l.append(st)
        q, k, v, lf, u, bg = project(xs, norm_mix[l], w_in[l], b_f[l])
        y_att = fox_attention_sample(q, k, v, lf, cache_k[l], cache_v[l], cache_logf[l], page_table)
        y_conv, st = short_conv(u, state_conv[l], conv_w[l])
        xs = merge_and_mlp(xs, y_att, y_conv, bg, norm_att_out[l], norm_conv_out[l], w_out[l],
                           norm_mlp[l], w_up[l], w_down[l])
        ks_l.append(k); vs_l.append(v); lfs_l.append(lf); cs_l.append(st)
    y_prompt = rmsnorm(xp, norm_final)
    y_sample = rmsnorm(xs, norm_final)
    return (y_prompt, y_sample,
            jnp.stack(kp_l), jnp.stack(vp_l), jnp.stack(lfp_l), jnp.stack(cp_l),
            jnp.stack(ks_l), jnp.stack(vs_l), jnp.stack(lfs_l), jnp.stack(cs_l))
```

```python
import functools

import jax
import jax.numpy as jnp
from jax import lax
from jax.experimental import pallas as pl
from jax.experimental.pallas import tpu as pltpu

F32 = jnp.float32
BF16 = jnp.bfloat16

ATT_HEADS = 16
HEAD_DIM = 64
ATT_DIM = ATT_HEADS * HEAD_DIM
CONV_DIM = 1024
CONV_W = 3
EPS = 1e-6
LANES = 128
HEAD_PAIRS = ATT_DIM // LANES
Q_SCALE = HEAD_DIM ** -0.5
NEG = -1e30
V7X_VMEM_LIMIT = 56 * 1024 * 1024


def _params(semantics, vmem_bytes):
    return pltpu.CompilerParams(dimension_semantics=semantics,
                                vmem_limit_bytes=min(int(vmem_bytes), V7X_VMEM_LIMIT))


def _dot(a, b):
    return jnp.dot(a, b, preferred_element_type=F32)


def _dot_nt(a, b):
    return lax.dot_general(a, b, (((1,), (1,)), ((), ())), preferred_element_type=F32)


def _split_bf16(x):
    hi = x.astype(BF16)
    lo = (x - hi.astype(F32)).astype(BF16)
    return hi, lo


def _upper_ones(n):
    r = lax.broadcasted_iota(jnp.int32, (n, n), 0)
    c = lax.broadcasted_iota(jnp.int32, (n, n), 1)
    return (r <= c).astype(BF16)


def _prefix_lanes(x, upper):
    hi, lo = _split_bf16(x)
    return _dot(hi, upper) + _dot(lo, upper)


def _rms_scale(x):
    return lax.rsqrt(jnp.mean(x * x, axis=-1, keepdims=True) + EPS)


def _log_sigmoid(z):
    return jnp.minimum(z, 0.0) - jnp.log1p(jnp.exp(-jnp.abs(z)))


def _qkv_kernel(x_ref, g_ref, w_ref, wflt_ref, bf_ref,
                q_ref, k_ref, v_ref, kb_ref, vb_ref, lft_ref, h_ref):
    j = pl.program_id(1)

    @pl.when(j == 0)
    def _():
        x = x_ref[...]
        hb = ((x * _rms_scale(x)) * g_ref[...]).astype(BF16)
        h_ref[...] = hb
        lft_ref[...] = _log_sigmoid(_dot_nt(wflt_ref[...], hb) + bf_ref[...])
        q = _dot(hb, w_ref[...]) * Q_SCALE
        for hp in range(HEAD_PAIRS):
            q_ref[hp] = q[:, hp * LANES:(hp + 1) * LANES].astype(BF16)

    def store_kv(full_ref, pair_ref):
        r = _dot(h_ref[...], w_ref[...])
        full_ref[...] = r
        for hp in range(HEAD_PAIRS):
            pair_ref[hp] = r[:, hp * LANES:(hp + 1) * LANES].astype(BF16)

    @pl.when(j == 1)
    def _():
        store_kv(k_ref, kb_ref)

    @pl.when(j == 2)
    def _():
        store_kv(v_ref, vb_ref)


def _qkv_proj(x, g, w_qkv, w_flt, b_f, tm):
    m, d = x.shape
    row = lambda i, j: (i, 0)
    pair = lambda i, j: (0, i, 0)
    pair_shape = jax.ShapeDtypeStruct((HEAD_PAIRS, m, LANES), BF16)
    full_shape = jax.ShapeDtypeStruct((m, ATT_DIM), F32)
    vmem = (2 * tm * d * 4 + 2 * d * ATT_DIM * 2 + 2 * 2 * tm * ATT_DIM * 4
            + 2 * 3 * tm * ATT_DIM * 2 + tm * d * 2 + 2 * tm * ATT_DIM * 4 + (4 << 20))
    return pl.pallas_call(
        _qkv_kernel,
        grid=(m // tm, 3),
        in_specs=[pl.BlockSpec((tm, d), row),
                  pl.BlockSpec((1, d), lambda i, j: (0, 0)),
                  pl.BlockSpec((d, ATT_DIM), lambda i, j: (0, j)),
                  pl.BlockSpec((ATT_HEADS, d), lambda i, j: (0, 0)),
                  pl.BlockSpec((ATT_HEADS, 1), lambda i, j: (0, 0))],
        out_specs=[pl.BlockSpec((HEAD_PAIRS, tm, LANES), pair),
                   pl.BlockSpec((tm, ATT_DIM), row),
                   pl.BlockSpec((tm, ATT_DIM), row),
                   pl.BlockSpec((HEAD_PAIRS, tm, LANES), pair),
                   pl.BlockSpec((HEAD_PAIRS, tm, LANES), pair),
                   pl.BlockSpec((ATT_HEADS, tm), lambda i, j: (0, i))],
        out_shape=[pair_shape, full_shape, full_shape, pair_shape, pair_shape,
                   jax.ShapeDtypeStruct((ATT_HEADS, m), F32)],
        scratch_shapes=[pltpu.VMEM((tm, d), BF16)],
        compiler_params=_params(("arbitrary", "arbitrary"), vmem),
        name="qkv_proj",
    )(x, g, w_qkv, w_flt, b_f)


def _attn_kernel(q_ref, k_ref, v_ref, lft_ref, g_ref, o_ref, negc_ref, acc_ref, *, tq, tk, seq):
    qi = pl.program_id(1)
    n_kv = seq // tk
    diag = tq // tk

    @pl.when(qi == 0)
    def _():
        upper = _upper_ones(tk)
        carry = jnp.zeros((ATT_HEADS, 1), F32)
        for kb in range(n_kv):
            c_blk = _prefix_lanes(lft_ref[:, kb * tk:(kb + 1) * tk], upper) + carry
            negc_ref[kb * ATT_HEADS:(kb + 1) * ATT_HEADS, :] = -c_blk
            carry = c_blk[:, tk - 1:tk]

    is_a = lax.broadcasted_iota(jnp.int32, (tq, LANES), 1) < HEAD_DIM
    rows = lax.broadcasted_iota(jnp.int32, (tq, tk), 0)
    cols = lax.broadcasted_iota(jnp.int32, (tq, tk), 1)

    def head_pair(hp, _):
        q2 = q_ref[hp]
        zero = jnp.zeros_like(q2)
        q_heads = (jnp.where(is_a, q2, zero), jnp.where(is_a, zero, q2))

        def tile(kv, carry, col_shift):
            start = pl.multiple_of(kv * tk, tk)
            k_blk = k_ref[hp, pl.ds(start, tk), :]
            v_blk = v_ref[hp, pl.ds(start, tk), :]
            out = []
            for a in range(2):
                m_old, l_old, acc = carry[a]
                s = _dot_nt(q_heads[a], k_blk) + negc_ref[pl.ds(kv * ATT_HEADS + 2 * hp + a, 1), :]
                if col_shift is not None:
                    s = jnp.where(cols + col_shift <= rows, s, NEG)
                m_new = jnp.maximum(m_old, jnp.max(s, axis=-1, keepdims=True))
                alpha = jnp.exp(m_old - m_new)
                p = jnp.exp(s - m_new)
                l_new = alpha * l_old + jnp.sum(p, axis=-1, keepdims=True)
                acc = alpha * acc + _dot(p.astype(BF16), v_blk)
                out.append((m_new, l_new, acc))
            return tuple(out)

        init = tuple((jnp.full((tq, 1), NEG, F32), jnp.zeros((tq, 1), F32),
                      jnp.zeros((tq, LANES), F32)) for _ in range(2))
        carry = lax.fori_loop(0, qi * diag, lambda kv, c: tile(kv, c, None), init)
        for d in range(diag):
            carry = tile(qi * diag + d, carry, d * tk)
        (_, l_a, acc_a), (_, l_b, acc_b) = carry
        acc_ref[hp] = jnp.where(is_a, acc_a / l_a, acc_b / l_b)
        return 0

    lax.fori_loop(0, HEAD_PAIRS, head_pair, 0)

    ss = jnp.zeros((tq, 1), F32)
    for hp in range(HEAD_PAIRS):
        o = acc_ref[hp]
        ss = ss + jnp.sum(o * o, axis=-1, keepdims=True)
    inv = lax.rsqrt(ss * (1.0 / ATT_DIM) + EPS)
    for hp in range(HEAD_PAIRS):
        sl = slice(hp * LANES, (hp + 1) * LANES)
        o_ref[:, sl] = ((acc_ref[hp] * inv) * g_ref[:, sl]).astype(BF16)


def _attn_prompt(q_hm, k_hm, v_hm, lft, g_att, batch, seq, tq, tk):
    m = batch * seq
    nq = seq // tq
    vmem = 2 * 2 * seq * ATT_DIM * 2 + 8 * tq * ATT_DIM * 4 + (8 << 20)
    return pl.pallas_call(
        functools.partial(_attn_kernel, tq=tq, tk=tk, seq=seq),
        grid=(batch, nq),
        in_specs=[pl.BlockSpec((HEAD_PAIRS, tq, LANES), lambda b, i: (0, b * nq + i, 0)),
                  pl.BlockSpec((HEAD_PAIRS, seq, LANES), lambda b, i: (0, b, 0)),
                  pl.BlockSpec((HEAD_PAIRS, seq, LANES), lambda b, i: (0, b, 0)),
                  pl.BlockSpec((ATT_HEADS, seq), lambda b, i: (0, b)),
                  pl.BlockSpec((1, ATT_DIM), lambda b, i: (0, 0))],
        out_specs=pl.BlockSpec((tq, ATT_DIM), lambda b, i: (b * nq + i, 0)),
        out_shape=jax.ShapeDtypeStruct((m, ATT_DIM), BF16),
        scratch_shapes=[pltpu.VMEM((seq // tk * ATT_HEADS, tk), F32),
                        pltpu.VMEM((HEAD_PAIRS, tq, LANES), F32)],
        compiler_params=_params(("arbitrary", "arbitrary"), vmem),
        name="attn_prompt",
    )(q_hm, k_hm, v_hm, lft, g_att)


def _sattn_kernel(pt_ref, q_ref, kn_ref, vn_ref, lfn_ref, g_ref, *refs, pps, dec):
    del pt_ref
    k_refs, v_refs, lf_refs = refs[:pps], refs[pps:2 * pps], refs[2 * pps:3 * pps]
    o_ref, qbd_ref, m_ref, l_ref, acc_ref, carry_ref = refs[3 * pps:]
    step = pl.program_id(1)
    rows_q = dec * ATT_HEADS
    page = k_refs[0].shape[1]

    @pl.when(step == 0)
    def _():
        q = q_ref[0].astype(F32)
        head_of_col = lax.broadcasted_iota(jnp.int32, (ATT_HEADS, ATT_DIM), 1) // HEAD_DIM
        head_of_row = lax.broadcasted_iota(jnp.int32, (ATT_HEADS, ATT_DIM), 0)
        own = head_of_col == head_of_row
        for t in range(dec):
            qt = jnp.broadcast_to(q[t:t + 1, :], (ATT_HEADS, ATT_DIM))
            qbd_ref[t * ATT_HEADS:(t + 1) * ATT_HEADS, :] = jnp.where(own, qt, 0.0)
        m_ref[...] = jnp.full(m_ref.shape, NEG, F32)
        l_ref[...] = jnp.zeros(l_ref.shape, F32)
        acc_ref[...] = jnp.zeros(acc_ref.shape, F32)
        carry_ref[...] = jnp.zeros(carry_ref.shape, F32)

    upper = _upper_ones(page)

    def update(scores, values):
        m_old = m_ref[...]
        m_new = m_old
        for s in scores:
            m_new = jnp.maximum(m_new, jnp.max(s, axis=-1, keepdims=True))
        alpha = jnp.exp(m_old - m_new)
        l_new = alpha * l_ref[...]
        pv = None
        for s, v in zip(scores, values):
            p = jnp.exp(s - m_new)
            l_new = l_new + jnp.sum(p, axis=-1, keepdims=True)
            d = _dot(p, v)
            pv = d if pv is None else pv + d
        acc_ref[...] = alpha * acc_ref[...] + pv
        l_ref[...] = l_new
        m_ref[...] = m_new

    def scores_of(k_page, c_page):
        return _dot_nt(qbd_ref[...], k_page) - jnp.tile(c_page, (dec, 1))

    lf_all = jnp.concatenate([lf_refs[r][0] for r in range(pps)], axis=0)
    within = _prefix_lanes(lf_all, upper)
    car = carry_ref[...]
    scores = []
    for r in range(pps):
        blk = within[r * ATT_HEADS:(r + 1) * ATT_HEADS, :]
        scores.append(scores_of(k_refs[r][0], blk + car))
        car = car + blk[:, page - 1:page]
    carry_ref[...] = car
    update(scores, [v_refs[r][0] for r in range(pps)])

    @pl.when(step == pl.num_programs(1) - 1)
    def _():
        c_new = _prefix_lanes(lfn_ref[0], upper) + carry_ref[...]
        s = scores_of(kn_ref[0], c_new)
        t_of_row = lax.broadcasted_iota(jnp.int32, (rows_q, page), 0) // ATT_HEADS
        key = lax.broadcasted_iota(jnp.int32, (rows_q, page), 1)
        update([jnp.where(key <= t_of_row, s, NEG)], [vn_ref[0]])
        o = acc_ref[...] / l_ref[...]
        head_of_col = lax.broadcasted_iota(jnp.int32, (rows_q, ATT_DIM), 1) // HEAD_DIM
        head_of_row = lax.broadcasted_iota(jnp.int32, (rows_q, ATT_DIM), 0) % ATT_HEADS
        o = jnp.where(head_of_col == head_of_row, o, 0.0)
        y = jnp.sum(o.reshape(dec, ATT_HEADS, ATT_DIM), axis=1)
        o_ref[0] = ((y * _rms_scale(y)) * g_ref[...]).astype(BF16)


def _attn_sample(page_table, q_s, k_new, v_new, lf_new, g_att, cache_k, cache_v, cache_lft, pps):
    nb, n_pages = page_table.shape
    dec = q_s.shape[1]
    page = cache_k.shape[1]
    rows_q = dec * ATT_HEADS
    steps = n_pages // pps

    def paged(r):
        return lambda b, s, pt: (pt[b, s * pps + r], 0, 0)

    per_seq = lambda b, s, pt: (b, 0, 0)
    in_specs = [pl.BlockSpec((1, dec, ATT_DIM), per_seq),
                pl.BlockSpec((1, page, ATT_DIM), per_seq),
                pl.BlockSpec((1, page, ATT_DIM), per_seq),
                pl.BlockSpec((1, ATT_HEADS, page), per_seq),
                pl.BlockSpec((1, ATT_DIM), lambda b, s, pt: (0, 0))]
    in_specs += [pl.BlockSpec((1, page, ATT_DIM), paged(r)) for r in range(pps)]
    in_specs += [pl.BlockSpec((1, page, ATT_DIM), paged(r)) for r in range(pps)]
    in_specs += [pl.BlockSpec((1, ATT_HEADS, page), paged(r)) for r in range(pps)]
    vmem = 2 * 2 * pps * page * ATT_DIM * 4 + 4 * page * ATT_DIM * 4 + 16 * rows_q * ATT_DIM * 4 + (8 << 20)
    grid_spec = pltpu.PrefetchScalarGridSpec(
        num_scalar_prefetch=1,
        grid=(nb, steps),
        in_specs=in_specs,
        out_specs=pl.BlockSpec((1, dec, ATT_DIM), per_seq),
        scratch_shapes=[pltpu.VMEM((rows_q, ATT_DIM), F32),
                        pltpu.VMEM((rows_q, 1), F32),
                        pltpu.VMEM((rows_q, 1), F32),
                        pltpu.VMEM((rows_q, ATT_DIM), F32),
                        pltpu.VMEM((ATT_HEADS, 1), F32)])
    return pl.pallas_call(
        functools.partial(_sattn_kernel, pps=pps, dec=dec),
        grid_spec=grid_spec,
        out_shape=jax.ShapeDtypeStruct((nb, dec, ATT_DIM), BF16),
        compiler_params=_params(("arbitrary", "arbitrary"), vmem),
        name="attn_sample",
    )(page_table, q_s, k_new, v_new, lf_new, g_att,
      *([cache_k] * pps), *([cache_v] * pps), *([cache_lft] * pps))


PAD_ROWS = 8


def _conv_kernel(*refs, tm, blocks_per_seq, period):
    if period is None:
        x_ref, g_ref, w_ref, cw_ref, gout_ref, o_ref, tail_ref, h_ref, s_ref, ubuf_ref = refs
    else:
        x_ref, g_ref, w_ref, cw_ref, gout_ref, f1_ref, f2_ref, o_ref, tail_ref, h_ref, s_ref, ubuf_ref = refs
    i = pl.program_id(0)
    j = pl.program_id(1)

    @pl.when(j == 0)
    def _():
        x = x_ref[...]
        hb = ((x * _rms_scale(x)) * g_ref[...]).astype(BF16)
        h_ref[...] = hb
        s_ref[...] = _dot(hb, w_ref[...])

    @pl.when(j == 1)
    def _():
        @pl.when(i % blocks_per_seq == 0)
        def _():
            ubuf_ref[0:PAD_ROWS, :] = jnp.zeros((PAD_ROWS, CONV_DIM), F32)

        u = _dot(h_ref[...], w_ref[...]) * s_ref[...]
        ubuf_ref[PAD_ROWS:PAD_ROWS + tm, :] = u
        u1 = ubuf_ref[PAD_ROWS - 1:PAD_ROWS - 1 + tm, :]
        u2 = ubuf_ref[PAD_ROWS - 2:PAD_ROWS - 2 + tm, :]
        if period is not None:
            pos = lax.broadcasted_iota(jnp.int32, (tm, CONV_DIM), 0) % period
            u1 = jnp.where(pos < 1, f1_ref[...], u1)
            u2 = jnp.where(pos < 2, f2_ref[...], u2)
        s_ref[...] = u2 * cw_ref[0:1, :] + u1 * cw_ref[1:2, :] + u * cw_ref[2:3, :]
        last = ubuf_ref[tm:tm + PAD_ROWS, :]
        ubuf_ref[0:PAD_ROWS, :] = last
        if period is None:
            @pl.when(i % blocks_per_seq == blocks_per_seq - 1)
            def _():
                tail_ref[0] = last
        else:
            tail_ref[...] = u

    @pl.when(j == 2)
    def _():
        y = _dot(h_ref[...], w_ref[...]) * s_ref[...]
        o_ref[...] = ((y * _rms_scale(y)) * gout_ref[...]).astype(BF16)


def _conv_proj(x, g, w_conv, conv_w, g_out, tm, seq, fills=None):
    m, d = x.shape
    row = lambda i, j: (i, 0)
    const = lambda i, j: (0, 0)
    in_specs = [pl.BlockSpec((tm, d), row),
                pl.BlockSpec((1, d), const),
                pl.BlockSpec((d, CONV_DIM), lambda i, j: (0, j)),
                pl.BlockSpec((CONV_W, CONV_DIM), const),
                pl.BlockSpec((1, CONV_DIM), const)]
    args = [x, g, w_conv, conv_w, g_out]
    if fills is None:
        blocks_per_seq = seq // tm
        tail_spec = pl.BlockSpec((1, PAD_ROWS, CONV_DIM), lambda i, j: (i // blocks_per_seq, 0, 0))
        tail_shape = jax.ShapeDtypeStruct((m // seq, PAD_ROWS, CONV_DIM), F32)
        period = None
    else:
        blocks_per_seq = 1
        in_specs += [pl.BlockSpec((tm, CONV_DIM), row), pl.BlockSpec((tm, CONV_DIM), row)]
        args += list(fills)
        tail_spec = pl.BlockSpec((tm, CONV_DIM), row)
        tail_shape = jax.ShapeDtypeStruct((m, CONV_DIM), F32)
        period = seq
    vmem = (2 * tm * d * 4 + 2 * d * CONV_DIM * 2 + tm * d * 2 + 8 * tm * CONV_DIM * 4 + (8 << 20))
    return pl.pallas_call(
        functools.partial(_conv_kernel, tm=tm, blocks_per_seq=blocks_per_seq, period=period),
        grid=(m // tm, 3),
        in_specs=in_specs,
        out_specs=[pl.BlockSpec((tm, CONV_DIM), row), tail_spec],
        out_shape=[jax.ShapeDtypeStruct((m, CONV_DIM), BF16), tail_shape],
        scratch_shapes=[pltpu.VMEM((tm, d), BF16),
                        pltpu.VMEM((tm, CONV_DIM), F32),
                        pltpu.VMEM((tm + PAD_ROWS, CONV_DIM), F32)],
        compiler_params=_params(("arbitrary", "arbitrary"), vmem),
        name="conv_proj",
    )(*args)


def _out_kernel(x_ref, ma_ref, mc_ref, wa_ref, wc_ref, g_ref, x1_ref, h2_ref):
    x1 = x_ref[...] + (_dot(ma_ref[...], wa_ref[...]) + _dot(mc_ref[...], wc_ref[...]))
    x1_ref[...] = x1
    h2_ref[...] = ((x1 * _rms_scale(x1)) * g_ref[...]).astype(BF16)


def _out_proj(x, m_att, m_conv, w_a, w_c, g_mlp, tm):
    m, d = x.shape
    row = lambda i: (i, 0)
    const = lambda i: (0, 0)
    vmem = 2 * 2 * tm * d * 4 + 2 * tm * d * 2 + 4 * tm * ATT_DIM * 2 + 2 * 2 * ATT_DIM * d * 2 + (8 << 20)
    return pl.pallas_call(
        _out_kernel,
        grid=(m // tm,),
        in_specs=[pl.BlockSpec((tm, d), row),
                  pl.BlockSpec((tm, ATT_DIM), row),
                  pl.BlockSpec((tm, CONV_DIM), row),
                  pl.BlockSpec((ATT_DIM, d), const),
                  pl.BlockSpec((CONV_DIM, d), const),
                  pl.BlockSpec((1, d), const)],
        out_specs=[pl.BlockSpec((tm, d), row), pl.BlockSpec((tm, d), row)],
        out_shape=[jax.ShapeDtypeStruct((m, d), F32), jax.ShapeDtypeStruct((m, d), BF16)],
        compiler_params=_params(("arbitrary",), vmem),
        name="out_proj",
    )(x, m_att, m_conv, w_a, w_c, g_mlp)


def _mlp_kernel(h_ref, x1_ref, wu_ref, wd_ref, g_ref, y_ref, acc_ref):
    f = pl.program_id(1)
    z = jnp.maximum(_dot(h_ref[...], wu_ref[...]), 0.0)
    part = _dot((z * z).astype(BF16), wd_ref[...])

    @pl.when(f == 0)
    def _():
        acc_ref[...] = part

    @pl.when(f != 0)
    def _():
        acc_ref[...] += part

    @pl.when(f == pl.num_programs(1) - 1)
    def _():
        x2 = x1_ref[...] + acc_ref[...]
        y_ref[...] = (x2 * _rms_scale(x2)) * g_ref[...]


def _mlp(h2, x1, w_up, w_down, g_final, tm, tf):
    m, d = x1.shape
    ff = w_up.shape[1]
    row = lambda i, f: (i, 0)
    vmem = (2 * tm * d * 2 + 2 * tm * d * 4 + 2 * 2 * d * tf * 2 + 2 * tm * d * 4 + tm * d * 4
            + 3 * tm * tf * 4 + (6 << 20))
    return pl.pallas_call(
        _mlp_kernel,
        grid=(m // tm, ff // tf),
        in_specs=[pl.BlockSpec((tm, d), row),
                  pl.BlockSpec((tm, d), row),
                  pl.BlockSpec((d, tf), lambda i, f: (0, f)),
                  pl.BlockSpec((tf, d), lambda i, f: (f, 0)),
                  pl.BlockSpec((1, d), lambda i, f: (0, 0))],
        out_specs=pl.BlockSpec((tm, d), row),
        out_shape=jax.ShapeDtypeStruct((m, d), F32),
        scratch_shapes=[pltpu.VMEM((tm, d), F32)],
        compiler_params=_params(("arbitrary", "arbitrary"), vmem),
        name="mlp",
    )(h2, x1, w_up, w_down, g_final)


def _row_tile(m, want):
    return want if m % want == 0 else m


def _layer_weights(norm_mix, w_in, b_f, conv_w, norm_att_out, norm_conv_out, w_out, norm_mlp,
                   w_up, w_down):
    a = ATT_DIM
    c0 = 3 * a + ATT_HEADS
    w_qkv = w_in[:, :3 * a].astype(BF16)
    w_flt = w_in[:, 3 * a:c0].T.astype(BF16)
    w_conv = jnp.concatenate([w_in[:, c0:c0 + CONV_DIM],
                              w_in[:, c0 + 2 * CONV_DIM:c0 + 3 * CONV_DIM],
                              w_in[:, c0 + CONV_DIM:c0 + 2 * CONV_DIM]],
                             axis=1).astype(BF16)
    return dict(g_mix=norm_mix[None, :], w_qkv=w_qkv, w_flt=w_flt, b_f=b_f[:, None],
                w_conv=w_conv, conv_w=conv_w, g_att=norm_att_out[None, :],
                g_conv=norm_conv_out[None, :], w_a=w_out[:a].astype(BF16),
                w_c=w_out[a:].astype(BF16), g_mlp=norm_mlp[None, :],
                w_up=w_up.astype(BF16), w_down=w_down.astype(BF16))


def _tail(x, m_att, m_conv, w, g_final, tm, tf):
    x1, h2 = _out_proj(x, m_att, m_conv, w["w_a"], w["w_c"], w["g_mlp"], tm)
    return _mlp(h2, x1, w["w_up"], w["w_down"], g_final, tm, _row_tile(w["w_up"].shape[1], tf))


def kernel(x_prompt, x_sample, cache_k, cache_v, cache_logf, state_conv, page_table, norm_mix, w_in,
           b_f, conv_w, norm_att_out, norm_conv_out, w_out, norm_mlp, w_up, w_down, norm_final):
    depth = w_in.shape[0]
    assert depth == 1, "single-layer step"
    batch, seq, d = x_prompt.shape
    nb, dec, _ = x_sample.shape
    page = cache_k.shape[2]
    w = _layer_weights(norm_mix[0], w_in[0], b_f[0], conv_w[0], norm_att_out[0], norm_conv_out[0],
                       w_out[0], norm_mlp[0], w_up[0], w_down[0])
    g_final = norm_final[None, :]

    xp = x_prompt.reshape(batch * seq, d)
    tm = _row_tile(seq, 512)
    q_hm, k_p, v_p, k_hm, v_hm, lft_p = _qkv_proj(xp, w["g_mix"], w["w_qkv"], w["w_flt"], w["b_f"], tm)
    tq = _row_tile(seq, 256)
    m_att = _attn_prompt(q_hm, k_hm, v_hm, lft_p, w["g_att"], batch, seq, tq, tq)
    m_conv, tail_p = _conv_proj(xp, w["g_mix"], w["w_conv"], w["conv_w"], w["g_conv"], tm, seq)
    y_p = _tail(xp, m_att, m_conv, w, g_final, tm, 1024)

    ms = nb * dec
    xs = x_sample.reshape(ms, d)
    q_s, k_s, v_s, _, _, lft_s = _qkv_proj(xs, w["g_mix"], w["w_qkv"], w["w_flt"], w["b_f"], ms)
    q_s = q_s.transpose(1, 0, 2).reshape(nb, dec, ATT_DIM)
    pad = ((0, 0), (0, page - dec), (0, 0))
    k_new = jnp.pad(k_s.reshape(nb, dec, ATT_DIM), pad)
    v_new = jnp.pad(v_s.reshape(nb, dec, ATT_DIM), pad)
    lf_new = jnp.pad(lft_s.reshape(ATT_HEADS, nb, dec).transpose(1, 0, 2),
                     ((0, 0), (0, 0), (0, page - dec)))
    n_pool = cache_k.shape[1]
    m_att_s = _attn_sample(page_table, q_s, k_new, v_new, lf_new, w["g_att"],
                           cache_k[0].reshape(n_pool, page, ATT_DIM),
                           cache_v[0].reshape(n_pool, page, ATT_DIM),
                           cache_logf[0].transpose(0, 2, 1), 8)
    st = state_conv[0]
    zeros = jnp.zeros((nb, dec - 1, CONV_DIM), F32)
    f1 = jnp.concatenate([st[:, 1:2], zeros], axis=1).reshape(ms, CONV_DIM)
    f2 = jnp.concatenate([st, zeros[:, 1:]], axis=1).reshape(ms, CONV_DIM)
    m_conv_s, u_s = _conv_proj(xs, w["g_mix"], w["w_conv"], w["conv_w"], w["g_conv"], ms, dec,
                               fills=(f1, f2))
    y_s = _tail(xs, m_att_s.reshape(ms, ATT_DIM), m_conv_s, w, g_final, ms, 1024)

    hd = (ATT_HEADS, HEAD_DIM)
    return (y_p.reshape(batch, seq, d),
            y_s.reshape(nb, dec, d),
            k_p.reshape(1, batch, seq, *hd),
            v_p.reshape(1, batch, seq, *hd),
            lft_p.T.reshape(1, batch, seq, ATT_HEADS),
            tail_p[:, PAD_ROWS - (CONV_W - 1):][None],
            k_s.reshape(1, nb, dec, *hd),
            v_s.reshape(1, nb, dec, *hd),
            lft_s.T.reshape(1, nb, dec, ATT_HEADS),
            u_s.reshape(nb, dec, CONV_DIM)[:, dec - (CONV_W - 1):][None])
```

```python
import functools

import jax
import jax.numpy as jnp
from jax import lax
from jax.experimental import pallas as pl
from jax.experimental.pallas import tpu as pltpu

F32 = jnp.float32
BF16 = jnp.bfloat16

ATT_HEADS = 16
HEAD_DIM = 64
ATT_DIM = ATT_HEADS * HEAD_DIM
CONV_DIM = 1024
CONV_W = 3
EPS = 1e-6
LANES = 128
HEAD_PAIRS = ATT_DIM // LANES
Q_SCALE = HEAD_DIM ** -0.5
NEG = -1e30
V7X_VMEM_LIMIT = 56 * 1024 * 1024


def _params(semantics, vmem_bytes):
    return pltpu.CompilerParams(dimension_semantics=semantics,
                                vmem_limit_bytes=min(int(vmem_bytes), V7X_VMEM_LIMIT))


def _dot(a, b):
    return jnp.dot(a, b, preferred_element_type=F32)


def _dot_nt(a, b):
    return lax.dot_general(a, b, (((1,), (1,)), ((), ())), preferred_element_type=F32)


def _split_bf16(x):
    hi = x.astype(BF16)
    lo = (x - hi.astype(F32)).astype(BF16)
    return hi, lo


def _upper_ones(n):
    r = lax.broadcasted_iota(jnp.int32, (n, n), 0)
    c = lax.broadcasted_iota(jnp.int32, (n, n), 1)
    return (r <= c).astype(BF16)


def _prefix_lanes(x, upper):
    hi, lo = _split_bf16(x)
    return _dot(hi, upper) + _dot(lo, upper)


def _rms_scale(x):
    return lax.rsqrt(jnp.mean(x * x, axis=-1, keepdims=True) + EPS)


def _log_sigmoid(z):
    return jnp.minimum(z, 0.0) - jnp.log1p(jnp.exp(-jnp.abs(z)))


def _qkv_kernel(x_ref, g_ref, w_ref, wflt_ref, bf_ref, pairs_ref, kv_ref, lft_ref, h_ref):
    j = pl.program_id(1)

    @pl.when(j == 0)
    def _():
        x = x_ref[...]
        hb = ((x * _rms_scale(x)) * g_ref[...]).astype(BF16)
        h_ref[...] = hb
        lft_ref[...] = _log_sigmoid(_dot_nt(wflt_ref[...], hb) + bf_ref[...])

    r = _dot(h_ref[...], w_ref[...])
    scale = jnp.where(j == 0, Q_SCALE, 1.0).astype(F32)
    for hp in range(HEAD_PAIRS):
        pairs_ref[hp] = (r[:, hp * LANES:(hp + 1) * LANES] * scale).astype(BF16)

    @pl.when(j > 0)
    def _():
        kv_ref[...] = r


def _qkv_proj(x, g, w_qkv, w_flt, b_f, tm):
    m, d = x.shape
    vmem = (2 * tm * d * 4 + 2 * d * ATT_DIM * 2 + 2 * tm * ATT_DIM * (4 + 2) + tm * d * 2
            + 2 * tm * ATT_DIM * 4 + (6 << 20))
    return pl.pallas_call(
        _qkv_kernel,
        grid=(m // tm, 3),
        in_specs=[pl.BlockSpec((tm, d), lambda i, j: (i, 0)),
                  pl.BlockSpec((1, d), lambda i, j: (0, 0)),
                  pl.BlockSpec((d, ATT_DIM), lambda i, j: (0, j)),
                  pl.BlockSpec((ATT_HEADS, d), lambda i, j: (0, 0)),
                  pl.BlockSpec((ATT_HEADS, 1), lambda i, j: (0, 0))],
        out_specs=[pl.BlockSpec((None, HEAD_PAIRS, tm, LANES), lambda i, j: (j, 0, i, 0)),
                   pl.BlockSpec((None, tm, ATT_DIM), lambda i, j: (jnp.maximum(j - 1, 0), i, 0)),
                   pl.BlockSpec((ATT_HEADS, tm), lambda i, j: (0, i))],
        out_shape=[jax.ShapeDtypeStruct((3, HEAD_PAIRS, m, LANES), BF16),
                   jax.ShapeDtypeStruct((2, m, ATT_DIM), F32),
                   jax.ShapeDtypeStruct((ATT_HEADS, m), F32)],
        scratch_shapes=[pltpu.VMEM((tm, d), BF16)],
        compiler_params=_params(("arbitrary", "arbitrary"), vmem),
        name="qkv_proj",
    )(x, g, w_qkv, w_flt, b_f)


def _attn_kernel(q_ref, k_ref, v_ref, lft_ref, g_ref, o_ref,
                 negc_ref, qx_ref, vx_ref, m_ref, acc_ref, *, tq, tk, seq):
    qi = pl.program_id(1)
    n_kv = seq // tk
    diag = tq // tk
    n_chunk = tk // LANES
    ones_col = (HEAD_DIM, 0)

    @pl.when(qi == 0)
    def _():
        upper = _upper_ones(tk)
        carry = jnp.zeros((ATT_HEADS, 1), F32)
        for kb in range(n_kv):
            c_blk = _prefix_lanes(lft_ref[:, kb * tk:(kb + 1) * tk], upper) + carry
            negc_ref[kb * ATT_HEADS:(kb + 1) * ATT_HEADS, :] = -c_blk
            carry = c_blk[:, tk - 1:tk]

        lane_k = lax.broadcasted_iota(jnp.int32, (tk, LANES), 1)

        def indicator(mask):
            return jnp.where(mask, 1.0, 0.0).astype(BF16)

        keep = (indicator(lane_k < HEAD_DIM), indicator(lane_k >= HEAD_DIM))
        ones = tuple(indicator(lane_k == ones_col[a]) for a in range(2))

        def build(kb, _):
            rows_k = pl.ds(pl.multiple_of(kb * tk, tk), tk)
            for hp in range(HEAD_PAIRS):
                v2 = v_ref[hp, rows_k, :]
                for a in range(2):
                    vx_ref[2 * hp + a, rows_k, :] = v2 * keep[a] + ones[a]
            return 0

        lax.fori_loop(0, n_kv, build, 0)

    is_a = lax.broadcasted_iota(jnp.int32, (tq, LANES), 1) < HEAD_DIM
    for hp in range(HEAD_PAIRS):
        q2 = q_ref[hp]
        zero = jnp.zeros_like(q2)
        qx_ref[hp, 0:tq, :] = jnp.where(is_a, q2, zero)
        qx_ref[hp, tq:2 * tq, :] = jnp.where(is_a, zero, q2)
    m_ref[...] = jnp.full(m_ref.shape, NEG, F32)
    acc_ref[...] = jnp.zeros(acc_ref.shape, F32)

    row = lax.broadcasted_iota(jnp.int32, (tq, LANES), 0)
    col = lax.broadcasted_iota(jnp.int32, (tq, LANES), 1)

    def kv_step(kv, masked):
        rows_k = pl.ds(pl.multiple_of(kv * tk, tk), tk)
        col_shift = kv * tk - qi * tq
        for hp in range(HEAD_PAIRS):
            s2 = _dot_nt(qx_ref[hp], k_ref[hp, rows_k, :])
            for a in range(2):
                h = 2 * hp + a
                bias = negc_ref[pl.ds(kv * ATT_HEADS + h, 1), :]
                chunks = []
                for c in range(n_chunk):
                    lanes = slice(c * LANES, (c + 1) * LANES)
                    sc = s2[a * tq:(a + 1) * tq, lanes] + bias[:, lanes]
                    if masked:
                        sc = jnp.where(col + (col_shift + c * LANES) <= row, sc, NEG)
                    chunks.append(sc)
                mx = chunks[0]
                for sc in chunks[1:]:
                    mx = jnp.maximum(mx, sc)
                m_old = m_ref[h]
                m_new = jnp.maximum(m_old, jnp.max(mx, axis=-1, keepdims=True))
                alpha = jnp.exp(m_old - m_new)
                p = jnp.concatenate([jnp.exp(sc - m_new) for sc in chunks], axis=1).astype(BF16)
                acc_ref[h] = alpha * acc_ref[h] + _dot(p, vx_ref[h, rows_k, :])
                m_ref[h] = m_new

    def full_step(kv, _):
        kv_step(kv, False)
        return 0

    def diag_step(kv, _):
        kv_step(kv, True)
        return 0

    lax.fori_loop(0, qi * diag, full_step, 0)
    lax.fori_loop(qi * diag, (qi + 1) * diag, diag_step, 0)

    outs = []
    ss = jnp.zeros((tq, 1), F32)
    for hp in range(HEAD_PAIRS):
        acc_a, acc_b = acc_ref[2 * hp], acc_ref[2 * hp + 1]
        o = jnp.where(is_a, acc_a / acc_a[:, ones_col[0]:ones_col[0] + 1],
                      acc_b / acc_b[:, ones_col[1]:ones_col[1] + 1])
        outs.append(o)
        ss = ss + jnp.sum(o * o, axis=-1, keepdims=True)
    inv = lax.rsqrt(ss * (1.0 / ATT_DIM) + EPS)
    for hp in range(HEAD_PAIRS):
        sl = slice(hp * LANES, (hp + 1) * LANES)
        o_ref[:, sl] = ((outs[hp] * inv) * g_ref[:, sl]).astype(BF16)


def _attn_prompt(qkv_pairs, lft, g_att, batch, seq, tq, tk):
    m = batch * seq
    nq = seq // tq
    vmem = (2 * 2 * seq * ATT_DIM * 2 + seq * ATT_DIM * 2 * 2 + 2 * ATT_HEADS * tq * LANES * 4
            + 4 * tq * ATT_DIM * 2 + (12 << 20))
    return pl.pallas_call(
        functools.partial(_attn_kernel, tq=tq, tk=tk, seq=seq),
        grid=(batch, nq),
        in_specs=[pl.BlockSpec((None, HEAD_PAIRS, tq, LANES), lambda b, i: (0, 0, b * nq + i, 0)),
                  pl.BlockSpec((None, HEAD_PAIRS, seq, LANES), lambda b, i: (1, 0, b, 0)),
                  pl.BlockSpec((None, HEAD_PAIRS, seq, LANES), lambda b, i: (2, 0, b, 0)),
                  pl.BlockSpec((ATT_HEADS, seq), lambda b, i: (0, b)),
                  pl.BlockSpec((1, ATT_DIM), lambda b, i: (0, 0))],
        out_specs=pl.BlockSpec((tq, ATT_DIM), lambda b, i: (b * nq + i, 0)),
        out_shape=jax.ShapeDtypeStruct((m, ATT_DIM), BF16),
        scratch_shapes=[pltpu.VMEM((seq // tk * ATT_HEADS, tk), F32),
                        pltpu.VMEM((HEAD_PAIRS, 2 * tq, LANES), BF16),
                        pltpu.VMEM((ATT_HEADS, seq, LANES), BF16),
                        pltpu.VMEM((ATT_HEADS, tq, LANES), F32),
                        pltpu.VMEM((ATT_HEADS, tq, LANES), F32)],
        compiler_params=_params(("arbitrary", "arbitrary"), vmem),
        name="attn_prompt",
    )(qkv_pairs, qkv_pairs, qkv_pairs, lft, g_att)


def _sattn_kernel(pt_ref, q_ref, kn_ref, vn_ref, lfn_ref, g_ref, *refs, pps, dec):
    del pt_ref
    k_refs, v_refs, lf_refs = refs[:pps], refs[pps:2 * pps], refs[2 * pps:3 * pps]
    o_ref, qbd_ref, m_ref, l_ref, acc_ref, carry_ref = refs[3 * pps:]
    step = pl.program_id(1)
    rows_q = dec * ATT_HEADS
    page = k_refs[0].shape[2]

    @pl.when(step == 0)
    def _():
        q = q_ref[0].astype(F32)
        head_of_col = lax.broadcasted_iota(jnp.int32, (ATT_HEADS, ATT_DIM), 1) // HEAD_DIM
        head_of_row = lax.broadcasted_iota(jnp.int32, (ATT_HEADS, ATT_DIM), 0)
        own = head_of_col == head_of_row
        for t in range(dec):
            qt = jnp.broadcast_to(q[t:t + 1, :], (ATT_HEADS, ATT_DIM))
            qbd_ref[t * ATT_HEADS:(t + 1) * ATT_HEADS, :] = jnp.where(own, qt, 0.0)
        m_ref[...] = jnp.full(m_ref.shape, NEG, F32)
        l_ref[...] = jnp.zeros(l_ref.shape, F32)
        acc_ref[...] = jnp.zeros(acc_ref.shape, F32)
        carry_ref[...] = jnp.zeros(carry_ref.shape, F32)

    upper = _upper_ones(page)

    def update(scores, values):
        m_old = m_ref[...]
        m_new = m_old
        for s in scores:
            m_new = jnp.maximum(m_new, jnp.max(s, axis=-1, keepdims=True))
        alpha = jnp.exp(m_old - m_new)
        l_new = alpha * l_ref[...]
        pv = None
        for s, v in zip(scores, values):
            p = jnp.exp(s - m_new)
            l_new = l_new + jnp.sum(p, axis=-1, keepdims=True)
            d = _dot_nt(p, v)
            pv = d if pv is None else pv + d
        acc_ref[...] = alpha * acc_ref[...] + pv
        l_ref[...] = l_new
        m_ref[...] = m_new

    def scores_of(k_page, c_page):
        return _dot(qbd_ref[...], k_page) - jnp.tile(c_page, (dec, 1))

    lf_all = jnp.concatenate([lf_refs[r][0] for r in range(pps)], axis=0)
    within = _prefix_lanes(lf_all, upper)
    car = carry_ref[...]
    scores = []
    for r in range(pps):
        blk = within[r * ATT_HEADS:(r + 1) * ATT_HEADS, :]
        scores.append(scores_of(k_refs[r][0], blk + car))
        car = car + blk[:, page - 1:page]
    carry_ref[...] = car
    update(scores, [v_refs[r][0] for r in range(pps)])

    @pl.when(step == pl.num_programs(1) - 1)
    def _():
        c_new = _prefix_lanes(lfn_ref[0], upper) + carry_ref[...]
        s = scores_of(kn_ref[0], c_new)
        t_of_row = lax.broadcasted_iota(jnp.int32, (rows_q, page), 0) // ATT_HEADS
        key = lax.broadcasted_iota(jnp.int32, (rows_q, page), 1)
        update([jnp.where(key <= t_of_row, s, NEG)], [vn_ref[0]])
        o = acc_ref[...] / l_ref[...]
        head_of_col = lax.broadcasted_iota(jnp.int32, (rows_q, ATT_DIM), 1) // HEAD_DIM
        head_of_row = lax.broadcasted_iota(jnp.int32, (rows_q, ATT_DIM), 0) % ATT_HEADS
        o = jnp.where(head_of_col == head_of_row, o, 0.0)
        y = jnp.sum(o.reshape(dec, ATT_HEADS, ATT_DIM), axis=1)
        o_ref[0] = ((y * _rms_scale(y)) * g_ref[...]).astype(BF16)


def _attn_sample(page_table, q_s, k_new, v_new, lf_new, g_att, cache_k, cache_v, cache_lft, pps):
    nb, n_pages = page_table.shape
    dec = q_s.shape[1]
    page = cache_k.shape[2]
    rows_q = dec * ATT_HEADS
    steps = n_pages // pps

    def paged(r):
        return lambda b, s, pt: (pt[b, s * pps + r], 0, 0)

    per_seq = lambda b, s, pt: (b, 0, 0)
    in_specs = [pl.BlockSpec((1, dec, ATT_DIM), per_seq),
                pl.BlockSpec((1, ATT_DIM, page), per_seq),
                pl.BlockSpec((1, ATT_DIM, page), per_seq),
                pl.BlockSpec((1, ATT_HEADS, page), per_seq),
                pl.BlockSpec((1, ATT_DIM), lambda b, s, pt: (0, 0))]
    in_specs += [pl.BlockSpec((1, ATT_DIM, page), paged(r)) for r in range(pps)]
    in_specs += [pl.BlockSpec((1, ATT_DIM, page), paged(r)) for r in range(pps)]
    in_specs += [pl.BlockSpec((1, ATT_HEADS, page), paged(r)) for r in range(pps)]
    vmem = 2 * 2 * pps * page * ATT_DIM * 4 + 4 * page * ATT_DIM * 4 + 16 * rows_q * ATT_DIM * 4 + (8 << 20)
    grid_spec = pltpu.PrefetchScalarGridSpec(
        num_scalar_prefetch=1,
        grid=(nb, steps),
        in_specs=in_specs,
        out_specs=pl.BlockSpec((1, dec, ATT_DIM), per_seq),
        scratch_shapes=[pltpu.VMEM((rows_q, ATT_DIM), F32),
                        pltpu.VMEM((rows_q, 1), F32),
                        pltpu.VMEM((rows_q, 1), F32),
                        pltpu.VMEM((rows_q, ATT_DIM), F32),
                        pltpu.VMEM((ATT_HEADS, 1), F32)])
    return pl.pallas_call(
        functools.partial(_sattn_kernel, pps=pps, dec=dec),
        grid_spec=grid_spec,
        out_shape=jax.ShapeDtypeStruct((nb, dec, ATT_DIM), BF16),
        compiler_params=_params(("arbitrary", "arbitrary"), vmem),
        name="attn_sample",
    )(page_table, q_s, k_new, v_new, lf_new, g_att,
      *([cache_k] * pps), *([cache_v] * pps), *([cache_lft] * pps))


PAD_ROWS = 8


def _conv_kernel(*refs, tm, blocks_per_seq, period):
    if period is None:
        x_ref, g_ref, w_ref, cw_ref, gout_ref, o_ref, tail_ref, h_ref, s_ref, ubuf_ref = refs
    else:
        x_ref, g_ref, w_ref, cw_ref, gout_ref, f1_ref, f2_ref, o_ref, tail_ref, h_ref, s_ref, ubuf_ref = refs
    i = pl.program_id(0)
    j = pl.program_id(1)

    @pl.when(j == 0)
    def _():
        x = x_ref[...]
        hb = ((x * _rms_scale(x)) * g_ref[...]).astype(BF16)
        h_ref[...] = hb
        s_ref[...] = _dot(hb, w_ref[...])

    @pl.when(j == 1)
    def _():
        @pl.when(i % blocks_per_seq == 0)
        def _():
            ubuf_ref[0:PAD_ROWS, :] = jnp.zeros((PAD_ROWS, CONV_DIM), F32)

        u = _dot(h_ref[...], w_ref[...]) * s_ref[...]
        ubuf_ref[PAD_ROWS:PAD_ROWS + tm, :] = u
        u1 = ubuf_ref[PAD_ROWS - 1:PAD_ROWS - 1 + tm, :]
        u2 = ubuf_ref[PAD_ROWS - 2:PAD_ROWS - 2 + tm, :]
        if period is not None:
            pos = lax.broadcasted_iota(jnp.int32, (tm, CONV_DIM), 0) % period
            u1 = jnp.where(pos < 1, f1_ref[...], u1)
            u2 = jnp.where(pos < 2, f2_ref[...], u2)
        s_ref[...] = u2 * cw_ref[0:1, :] + u1 * cw_ref[1:2, :] + u * cw_ref[2:3, :]
        last = ubuf_ref[tm:tm + PAD_ROWS, :]
        ubuf_ref[0:PAD_ROWS, :] = last
        if period is None:
            @pl.when(i % blocks_per_seq == blocks_per_seq - 1)
            def _():
                tail_ref[0] = last
        else:
            tail_ref[...] = u

    @pl.when(j == 2)
    def _():
        y = _dot(h_ref[...], w_ref[...]) * s_ref[...]
        o_ref[...] = ((y * _rms_scale(y)) * gout_ref[...]).astype(BF16)


def _conv_proj(x, g, w_conv, conv_w, g_out, tm, seq, fills=None):
    m, d = x.shape
    row = lambda i, j: (i, 0)
    const = lambda i, j: (0, 0)
    in_specs = [pl.BlockSpec((tm, d), row),
                pl.BlockSpec((1, d), const),
                pl.BlockSpec((d, CONV_DIM), lambda i, j: (0, j)),
                pl.BlockSpec((CONV_W, CONV_DIM), const),
                pl.BlockSpec((1, CONV_DIM), const)]
    args = [x, g, w_conv, conv_w, g_out]
    if fills is None:
        blocks_per_seq = seq // tm
        tail_spec = pl.BlockSpec((1, PAD_ROWS, CONV_DIM), lambda i, j: (i // blocks_per_seq, 0, 0))
        tail_shape = jax.ShapeDtypeStruct((m // seq, PAD_ROWS, CONV_DIM), F32)
        period = None
    else:
        blocks_per_seq = 1
        in_specs += [pl.BlockSpec((tm, CONV_DIM), row), pl.BlockSpec((tm, CONV_DIM), row)]
        args += list(fills)
        tail_spec = pl.BlockSpec((tm, CONV_DIM), row)
        tail_shape = jax.ShapeDtypeStruct((m, CONV_DIM), F32)
        period = seq
    vmem = (2 * tm * d * 4 + 2 * d * CONV_DIM * 2 + tm * d * 2 + 8 * tm * CONV_DIM * 4 + (8 << 20))
    return pl.pallas_call(
        functools.partial(_conv_kernel, tm=tm, blocks_per_seq=blocks_per_seq, period=period),
        grid=(m // tm, 3),
        in_specs=in_specs,
        out_specs=[pl.BlockSpec((tm, CONV_DIM), row), tail_spec],
        out_shape=[jax.ShapeDtypeStruct((m, CONV_DIM), BF16), tail_shape],
        scratch_shapes=[pltpu.VMEM((tm, d), BF16),
                        pltpu.VMEM((tm, CONV_DIM), F32),
                        pltpu.VMEM((tm + PAD_ROWS, CONV_DIM), F32)],
        compiler_params=_params(("arbitrary", "arbitrary"), vmem),
        name="conv_proj",
    )(*args)


def _out_kernel(x_ref, ma_ref, mc_ref, wa_ref, wc_ref, g_ref, x1_ref, h2_ref):
    x1 = x_ref[...] + (_dot(ma_ref[...], wa_ref[...]) + _dot(mc_ref[...], wc_ref[...]))
    x1_ref[...] = x1
    h2_ref[...] = ((x1 * _rms_scale(x1)) * g_ref[...]).astype(BF16)


def _out_proj(x, m_att, m_conv, w_a, w_c, g_mlp, tm):
    m, d = x.shape
    row = lambda i: (i, 0)
    const = lambda i: (0, 0)
    vmem = 2 * 2 * tm * d * 4 + 2 * tm * d * 2 + 4 * tm * ATT_DIM * 2 + 2 * 2 * ATT_DIM * d * 2 + (8 << 20)
    return pl.pallas_call(
        _out_kernel,
        grid=(m // tm,),
        in_specs=[pl.BlockSpec((tm, d), row),
                  pl.BlockSpec((tm, ATT_DIM), row),
                  pl.BlockSpec((tm, CONV_DIM), row),
                  pl.BlockSpec((ATT_DIM, d), const),
                  pl.BlockSpec((CONV_DIM, d), const),
                  pl.BlockSpec((1, d), const)],
        out_specs=[pl.BlockSpec((tm, d), row), pl.BlockSpec((tm, d), row)],
        out_shape=[jax.ShapeDtypeStruct((m, d), F32), jax.ShapeDtypeStruct((m, d), BF16)],
        compiler_params=_params(("arbitrary",), vmem),
        name="out_proj",
    )(x, m_att, m_conv, w_a, w_c, g_mlp)


def _mlp_kernel(h_ref, x1_ref, wu_ref, wd_ref, g_ref, y_ref, acc_ref):
    f = pl.program_id(1)
    z = jnp.maximum(_dot(h_ref[...], wu_ref[...]), 0.0)
    part = _dot((z * z).astype(BF16), wd_ref[...])

    @pl.when(f == 0)
    def _():
        acc_ref[...] = part

    @pl.when(f != 0)
    def _():
        acc_ref[...] += part

    @pl.when(f == pl.num_programs(1) - 1)
    def _():
        x2 = x1_ref[...] + acc_ref[...]
        y_ref[...] = (x2 * _rms_scale(x2)) * g_ref[...]


def _mlp(h2, x1, w_up, w_down, g_final, tm, tf):
    m, d = x1.shape
    ff = w_up.shape[1]
    row = lambda i, f: (i, 0)
    vmem = (2 * tm * d * 2 + 2 * tm * d * 4 + 2 * 2 * d * tf * 2 + 2 * tm * d * 4 + tm * d * 4
            + 3 * tm * tf * 4 + (6 << 20))
    return pl.pallas_call(
        _mlp_kernel,
        grid=(m // tm, ff // tf),
        in_specs=[pl.BlockSpec((tm, d), row),
                  pl.BlockSpec((tm, d), row),
                  pl.BlockSpec((d, tf), lambda i, f: (0, f)),
                  pl.BlockSpec((tf, d), lambda i, f: (f, 0)),
                  pl.BlockSpec((1, d), lambda i, f: (0, 0))],
        out_specs=pl.BlockSpec((tm, d), row),
        out_shape=jax.ShapeDtypeStruct((m, d), F32),
        scratch_shapes=[pltpu.VMEM((tm, d), F32)],
        compiler_params=_params(("arbitrary", "arbitrary"), vmem),
        name="mlp",
    )(h2, x1, w_up, w_down, g_final)


def _keys_last(cache):
    n_pool, page = cache.shape[:2]
    return cache.transpose(0, 2, 3, 1).reshape(n_pool, ATT_DIM, page)


def _row_tile(m, want):
    return want if m % want == 0 else m


def _layer_weights(norm_mix, w_in, b_f, conv_w, norm_att_out, norm_conv_out, w_out, norm_mlp,
                   w_up, w_down):
    a = ATT_DIM
    c0 = 3 * a + ATT_HEADS
    w_qkv = w_in[:, :3 * a].astype(BF16)
    w_flt = w_in[:, 3 * a:c0].T.astype(BF16)
    w_conv = jnp.concatenate([w_in[:, c0:c0 + CONV_DIM],
                              w_in[:, c0 + 2 * CONV_DIM:c0 + 3 * CONV_DIM],
                              w_in[:, c0 + CONV_DIM:c0 + 2 * CONV_DIM]],
                             axis=1).astype(BF16)
    return dict(g_mix=norm_mix[None, :], w_qkv=w_qkv, w_flt=w_flt, b_f=b_f[:, None],
                w_conv=w_conv, conv_w=conv_w, g_att=norm_att_out[None, :],
                g_conv=norm_conv_out[None, :], w_a=w_out[:a].astype(BF16),
                w_c=w_out[a:].astype(BF16), g_mlp=norm_mlp[None, :],
                w_up=w_up.astype(BF16), w_down=w_down.astype(BF16))


def _tail(x, m_att, m_conv, w, g_final, tm, tf):
    x1, h2 = _out_proj(x, m_att, m_conv, w["w_a"], w["w_c"], w["g_mlp"], tm)
    return _mlp(h2, x1, w["w_up"], w["w_down"], g_final, tm, _row_tile(w["w_up"].shape[1], tf))


def kernel(x_prompt, x_sample, cache_k, cache_v, cache_logf, state_conv, page_table, norm_mix, w_in,
           b_f, conv_w, norm_att_out, norm_conv_out, w_out, norm_mlp, w_up, w_down, norm_final):
    depth = w_in.shape[0]
    assert depth == 1, "single-layer step"
    batch, seq, d = x_prompt.shape
    nb, dec, _ = x_sample.shape
    page = cache_k.shape[2]
    w = _layer_weights(norm_mix[0], w_in[0], b_f[0], conv_w[0], norm_att_out[0], norm_conv_out[0],
                       w_out[0], norm_mlp[0], w_up[0], w_down[0])
    g_final = norm_final[None, :]

    xp = x_prompt.reshape(batch * seq, d)
    tm_proj = _row_tile(seq, 1024)
    pairs_p, kv_p, lft_p = _qkv_proj(xp, w["g_mix"], w["w_qkv"], w["w_flt"], w["b_f"], tm_proj)
    k_p, v_p = kv_p[0], kv_p[1]
    tq = _row_tile(seq, 256)
    m_att = _attn_prompt(pairs_p, lft_p, w["g_att"], batch, seq, tq, tq)
    m_conv, tail_p = _conv_proj(xp, w["g_mix"], w["w_conv"], w["conv_w"], w["g_conv"], tm_proj, seq)
    y_p = _tail(xp, m_att, m_conv, w, g_final, _row_tile(seq, 512), 1024)

    ms = nb * dec
    xs = x_sample.reshape(ms, d)
    pairs_s, kv_s, lft_s = _qkv_proj(xs, w["g_mix"], w["w_qkv"], w["w_flt"], w["b_f"], ms)
    k_s, v_s = kv_s[0], kv_s[1]
    q_s = pairs_s[0].transpose(1, 0, 2).reshape(nb, dec, ATT_DIM)
    pad = ((0, 0), (0, 0), (0, page - dec))
    k_new = jnp.pad(k_s.reshape(nb, dec, ATT_DIM).transpose(0, 2, 1), pad)
    v_new = jnp.pad(v_s.reshape(nb, dec, ATT_DIM).transpose(0, 2, 1), pad)
    lf_new = jnp.pad(lft_s.reshape(ATT_HEADS, nb, dec).transpose(1, 0, 2), pad)
    n_pool = cache_k.shape[1]
    m_att_s = _attn_sample(page_table, q_s, k_new, v_new, lf_new, w["g_att"],
                           _keys_last(cache_k[0]), _keys_last(cache_v[0]),
                           cache_logf[0].transpose(0, 2, 1), 8)
    st = state_conv[0]
    zeros = jnp.zeros((nb, dec - 1, CONV_DIM), F32)
    f1 = jnp.concatenate([st[:, 1:2], zeros], axis=1).reshape(ms, CONV_DIM)
    f2 = jnp.concatenate([st, zeros[:, 1:]], axis=1).reshape(ms, CONV_DIM)
    m_conv_s, u_s = _conv_proj(xs, w["g_mix"], w["w_conv"], w["conv_w"], w["g_conv"], ms, dec,
                               fills=(f1, f2))
    y_s = _tail(xs, m_att_s.reshape(ms, ATT_DIM), m_conv_s, w, g_final, ms, 1024)

    hd = (ATT_HEADS, HEAD_DIM)
    return (y_p.reshape(batch, seq, d),
            y_s.reshape(nb, dec, d),
            k_p.reshape(1, batch, seq, *hd),
            v_p.reshape(1, batch, seq, *hd),
            lft_p.T.reshape(1, batch, seq, ATT_HEADS),
            tail_p[:, PAD_ROWS - (CONV_W - 1):][None],
            k_s.reshape(1, nb, dec, *hd),
            v_s.reshape(1, nb, dec, *hd),
            lft_s.T.reshape(1, nb, dec, ATT_HEADS),
            u_s.reshape(nb, dec, CONV_DIM)[:, dec - (CONV_W - 1):][None])
```

```python
import functools

import jax
import jax.numpy as jnp
from jax import lax
from jax.experimental import pallas as pl
from jax.experimental.pallas import tpu as pltpu

F32 = jnp.float32
BF16 = jnp.bfloat16

ATT_HEADS = 16
HEAD_DIM = 64
ATT_DIM = ATT_HEADS * HEAD_DIM
CONV_DIM = 1024
CONV_W = 3
EPS = 1e-6
LANES = 128
HEAD_PAIRS = ATT_DIM // LANES
Q_SCALE = HEAD_DIM ** -0.5
NEG = -1e30
V7X_VMEM_LIMIT = 56 * 1024 * 1024


def _params(semantics, vmem_bytes):
    return pltpu.CompilerParams(dimension_semantics=semantics,
                                vmem_limit_bytes=min(int(vmem_bytes), V7X_VMEM_LIMIT))


def _dot(a, b):
    return jnp.dot(a, b, preferred_element_type=F32)


def _dot_nt(a, b):
    return lax.dot_general(a, b, (((1,), (1,)), ((), ())), preferred_element_type=F32)


def _split_bf16(x):
    hi = x.astype(BF16)
    lo = (x - hi.astype(F32)).astype(BF16)
    return hi, lo


def _upper_ones(n):
    r = lax.broadcasted_iota(jnp.int32, (n, n), 0)
    c = lax.broadcasted_iota(jnp.int32, (n, n), 1)
    return (r <= c).astype(BF16)


def _prefix_lanes(x, upper):
    hi, lo = _split_bf16(x)
    return _dot(hi, upper) + _dot(lo, upper)


def _rms_scale(x):
    return lax.rsqrt(jnp.mean(x * x, axis=-1, keepdims=True) + EPS)


def _log_sigmoid(z):
    return jnp.minimum(z, 0.0) - jnp.log1p(jnp.exp(-jnp.abs(z)))


def _qkv_kernel(x_ref, g_ref, w_ref, wflt_ref, bf_ref, pairs_ref, k_ref, v_ref, lft_ref):
    x = x_ref[...]
    tm = x.shape[0]
    hb = ((x * _rms_scale(x)) * g_ref[...]).astype(BF16)
    lft_ref[...] = _log_sigmoid(_dot_nt(wflt_ref[...], hb) + bf_ref[...])
    for j, heads_ref in enumerate((None, k_ref, v_ref)):
        r = _dot(hb, w_ref[:, j * ATT_DIM:(j + 1) * ATT_DIM])
        scaled = r * Q_SCALE if j == 0 else r
        for hp in range(HEAD_PAIRS):
            pairs_ref[j, hp] = scaled[:, hp * LANES:(hp + 1) * LANES].astype(BF16)
        if heads_ref is not None:
            for h in range(ATT_HEADS):
                heads_ref[pl.ds(h, tm, stride=ATT_HEADS), :] = r[:, h * HEAD_DIM:(h + 1) * HEAD_DIM]


def _resident(shape):
    return pl.BlockSpec(shape, lambda i: (0,) * len(shape), pipeline_mode=pl.Buffered(1))


def _qkv_proj(x, g, w_qkv, w_flt, b_f, tm):
    m, d = x.shape
    heads_rows = tm * ATT_HEADS * LANES * 4
    vmem = (2 * tm * d * 4 + d * 3 * ATT_DIM * 2 + 2 * 3 * tm * ATT_DIM * 2 + 2 * 2 * heads_rows
            + tm * d * 2 + 4 * tm * ATT_DIM * 4 + (6 << 20))
    heads_spec = pl.BlockSpec((tm * ATT_HEADS, HEAD_DIM), lambda i: (i, 0))
    heads_shape = jax.ShapeDtypeStruct((m * ATT_HEADS, HEAD_DIM), F32)
    return pl.pallas_call(
        _qkv_kernel,
        grid=(m // tm,),
        in_specs=[pl.BlockSpec((tm, d), lambda i: (i, 0)),
                  _resident((1, d)),
                  _resident((d, 3 * ATT_DIM)),
                  _resident((ATT_HEADS, d)),
                  _resident((ATT_HEADS, 1))],
        out_specs=[pl.BlockSpec((3, HEAD_PAIRS, tm, LANES), lambda i: (0, 0, i, 0)),
                   heads_spec, heads_spec,
                   pl.BlockSpec((ATT_HEADS, tm), lambda i: (0, i))],
        out_shape=[jax.ShapeDtypeStruct((3, HEAD_PAIRS, m, LANES), BF16),
                   heads_shape, heads_shape,
                   jax.ShapeDtypeStruct((ATT_HEADS, m), F32)],
        compiler_params=_params(("arbitrary",), vmem),
        name="qkv_proj",
    )(x, g, w_qkv, w_flt, b_f)


def _attn_kernel(q_ref, k_ref, v_ref, lft_ref, g_ref, o_ref,
                 negc_ref, qx_ref, vx_ref, m_ref, acc_ref, *, tq, tk, seq):
    qi = pl.program_id(1)
    n_kv = seq // tk
    diag = tq // tk
    n_chunk = tk // LANES
    ones_col = (HEAD_DIM, 0)

    @pl.when(qi == 0)
    def _():
        upper = _upper_ones(tk)
        carry = jnp.zeros((ATT_HEADS, 1), F32)
        for kb in range(n_kv):
            c_blk = _prefix_lanes(lft_ref[:, kb * tk:(kb + 1) * tk], upper) + carry
            negc_ref[kb * ATT_HEADS:(kb + 1) * ATT_HEADS, :] = -c_blk
            carry = c_blk[:, tk - 1:tk]

        lane_k = lax.broadcasted_iota(jnp.int32, (tk, LANES), 1)

        def indicator(mask):
            return jnp.where(mask, 1.0, 0.0).astype(BF16)

        keep = (indicator(lane_k < HEAD_DIM), indicator(lane_k >= HEAD_DIM))
        ones = tuple(indicator(lane_k == ones_col[a]) for a in range(2))

        def build(kb, _):
            rows_k = pl.ds(pl.multiple_of(kb * tk, tk), tk)
            for hp in range(HEAD_PAIRS):
                v2 = v_ref[hp, rows_k, :]
                for a in range(2):
                    vx_ref[2 * hp + a, rows_k, :] = v2 * keep[a] + ones[a]
            return 0

        lax.fori_loop(0, n_kv, build, 0)

    is_a = lax.broadcasted_iota(jnp.int32, (tq, LANES), 1) < HEAD_DIM
    for hp in range(HEAD_PAIRS):
        q2 = q_ref[hp]
        zero = jnp.zeros_like(q2)
        qx_ref[hp, 0:tq, :] = jnp.where(is_a, q2, zero)
        qx_ref[hp, tq:2 * tq, :] = jnp.where(is_a, zero, q2)
    m_ref[...] = jnp.full(m_ref.shape, NEG, F32)
    acc_ref[...] = jnp.zeros(acc_ref.shape, F32)

    row = lax.broadcasted_iota(jnp.int32, (tq, LANES), 0)
    col = lax.broadcasted_iota(jnp.int32, (tq, LANES), 1)

    def kv_step(kv, masked):
        rows_k = pl.ds(pl.multiple_of(kv * tk, tk), tk)
        col_shift = kv * tk - qi * tq
        for hp in range(HEAD_PAIRS):
            s2 = _dot_nt(qx_ref[hp], k_ref[hp, rows_k, :])
            for a in range(2):
                h = 2 * hp + a
                bias = negc_ref[pl.ds(kv * ATT_HEADS + h, 1), :]
                chunks = []
                for c in range(n_chunk):
                    lanes = slice(c * LANES, (c + 1) * LANES)
                    sc = s2[a * tq:(a + 1) * tq, lanes] + bias[:, lanes]
                    if masked:
                        sc = jnp.where(col + (col_shift + c * LANES) <= row, sc, NEG)
                    chunks.append(sc)
                mx = chunks[0]
                for sc in chunks[1:]:
                    mx = jnp.maximum(mx, sc)
                m_old = m_ref[h]
                m_new = jnp.maximum(m_old, jnp.max(mx, axis=-1, keepdims=True))
                alpha = jnp.exp(m_old - m_new)
                p = jnp.concatenate([jnp.exp(sc - m_new) for sc in chunks], axis=1).astype(BF16)
                acc_ref[h] = alpha * acc_ref[h] + _dot(p, vx_ref[h, rows_k, :])
                m_ref[h] = m_new

    def full_step(kv, _):
        kv_step(kv, False)
        return 0

    def diag_step(kv, _):
        kv_step(kv, True)
        return 0

    lax.fori_loop(0, qi * diag, full_step, 0)
    lax.fori_loop(qi * diag, (qi + 1) * diag, diag_step, 0)

    outs = []
    ss = jnp.zeros((tq, 1), F32)
    for hp in range(HEAD_PAIRS):
        acc_a, acc_b = acc_ref[2 * hp], acc_ref[2 * hp + 1]
        o = jnp.where(is_a, acc_a / acc_a[:, ones_col[0]:ones_col[0] + 1],
                      acc_b / acc_b[:, ones_col[1]:ones_col[1] + 1])
        outs.append(o)
        ss = ss + jnp.sum(o * o, axis=-1, keepdims=True)
    inv = lax.rsqrt(ss * (1.0 / ATT_DIM) + EPS)
    for hp in range(HEAD_PAIRS):
        sl = slice(hp * LANES, (hp + 1) * LANES)
        o_ref[:, sl] = ((outs[hp] * inv) * g_ref[:, sl]).astype(BF16)


def _attn_prompt(qkv_pairs, lft, g_att, batch, seq, tq, tk):
    m = batch * seq
    nq = seq // tq
    vmem = (2 * 2 * seq * ATT_DIM * 2 + seq * ATT_DIM * 2 * 2 + 2 * ATT_HEADS * tq * LANES * 4
            + 4 * tq * ATT_DIM * 2 + (12 << 20))
    return pl.pallas_call(
        functools.partial(_attn_kernel, tq=tq, tk=tk, seq=seq),
        grid=(batch, nq),
        in_specs=[pl.BlockSpec((None, HEAD_PAIRS, tq, LANES), lambda b, i: (0, 0, b * nq + i, 0)),
                  pl.BlockSpec((None, HEAD_PAIRS, seq, LANES), lambda b, i: (1, 0, b, 0)),
                  pl.BlockSpec((None, HEAD_PAIRS, seq, LANES), lambda b, i: (2, 0, b, 0)),
                  pl.BlockSpec((ATT_HEADS, seq), lambda b, i: (0, b)),
                  pl.BlockSpec((1, ATT_DIM), lambda b, i: (0, 0))],
        out_specs=pl.BlockSpec((tq, ATT_DIM), lambda b, i: (b * nq + i, 0)),
        out_shape=jax.ShapeDtypeStruct((m, ATT_DIM), BF16),
        scratch_shapes=[pltpu.VMEM((seq // tk * ATT_HEADS, tk), F32),
                        pltpu.VMEM((HEAD_PAIRS, 2 * tq, LANES), BF16),
                        pltpu.VMEM((ATT_HEADS, seq, LANES), BF16),
                        pltpu.VMEM((ATT_HEADS, tq, LANES), F32),
                        pltpu.VMEM((ATT_HEADS, tq, LANES), F32)],
        compiler_params=_params(("arbitrary", "arbitrary"), vmem),
        name="attn_prompt",
    )(qkv_pairs, qkv_pairs, qkv_pairs, lft, g_att)


def _sattn_kernel(pt_ref, q_ref, kn_ref, vn_ref, lfn_ref, g_ref, *refs, pps, dec):
    del pt_ref
    k_refs, v_refs, lf_refs = refs[:pps], refs[pps:2 * pps], refs[2 * pps:3 * pps]
    o_ref, qbd_ref, m_ref, l_ref, acc_ref, carry_ref = refs[3 * pps:]
    step = pl.program_id(1)
    rows_q = dec * ATT_HEADS
    page = k_refs[0].shape[2]

    @pl.when(step == 0)
    def _():
        q = q_ref[0].astype(F32)
        head_of_col = lax.broadcasted_iota(jnp.int32, (ATT_HEADS, ATT_DIM), 1) // HEAD_DIM
        head_of_row = lax.broadcasted_iota(jnp.int32, (ATT_HEADS, ATT_DIM), 0)
        own = head_of_col == head_of_row
        for t in range(dec):
            qt = jnp.broadcast_to(q[t:t + 1, :], (ATT_HEADS, ATT_DIM))
            qbd_ref[t * ATT_HEADS:(t + 1) * ATT_HEADS, :] = jnp.where(own, qt, 0.0)
        m_ref[...] = jnp.full(m_ref.shape, NEG, F32)
        l_ref[...] = jnp.zeros(l_ref.shape, F32)
        acc_ref[...] = jnp.zeros(acc_ref.shape, F32)
        carry_ref[...] = jnp.zeros(carry_ref.shape, F32)

    upper = _upper_ones(page)

    def update(scores, values):
        m_old = m_ref[...]
        m_new = m_old
        for s in scores:
            m_new = jnp.maximum(m_new, jnp.max(s, axis=-1, keepdims=True))
        alpha = jnp.exp(m_old - m_new)
        l_new = alpha * l_ref[...]
        pv = None
        for s, v in zip(scores, values):
            p = jnp.exp(s - m_new)
            l_new = l_new + jnp.sum(p, axis=-1, keepdims=True)
            d = _dot_nt(p, v)
            pv = d if pv is None else pv + d
        acc_ref[...] = alpha * acc_ref[...] + pv
        l_ref[...] = l_new
        m_ref[...] = m_new

    def scores_of(k_page, c_page):
        return _dot(qbd_ref[...], k_page) - jnp.tile(c_page, (dec, 1))

    lf_all = jnp.concatenate([lf_refs[r][0] for r in range(pps)], axis=0)
    within = _prefix_lanes(lf_all, upper)
    car = carry_ref[...]
    scores = []
    for r in range(pps):
        blk = within[r * ATT_HEADS:(r + 1) * ATT_HEADS, :]
        scores.append(scores_of(k_refs[r][0], blk + car))
        car = car + blk[:, page - 1:page]
    carry_ref[...] = car
    update(scores, [v_refs[r][0] for r in range(pps)])

    @pl.when(step == pl.num_programs(1) - 1)
    def _():
        c_new = _prefix_lanes(lfn_ref[0], upper) + carry_ref[...]
        s = scores_of(kn_ref[0], c_new)
        t_of_row = lax.broadcasted_iota(jnp.int32, (rows_q, page), 0) // ATT_HEADS
        key = lax.broadcasted_iota(jnp.int32, (rows_q, page), 1)
        update([jnp.where(key <= t_of_row, s, NEG)], [vn_ref[0]])
        o = acc_ref[...] / l_ref[...]
        head_of_col = lax.broadcasted_iota(jnp.int32, (rows_q, ATT_DIM), 1) // HEAD_DIM
        head_of_row = lax.broadcasted_iota(jnp.int32, (rows_q, ATT_DIM), 0) % ATT_HEADS
        o = jnp.where(head_of_col == head_of_row, o, 0.0)
        y = jnp.sum(o.reshape(dec, ATT_HEADS, ATT_DIM), axis=1)
        o_ref[0] = ((y * _rms_scale(y)) * g_ref[...]).astype(BF16)


SAMPLE_PAGES_PER_STEP = 16


def _attn_sample(page_table, q_s, k_new, v_new, lf_new, g_att, cache_k, cache_v, cache_lft, pps):
    nb, n_pages = page_table.shape
    pps = pps if n_pages % pps == 0 else n_pages
    dec = q_s.shape[1]
    page = cache_k.shape[2]
    rows_q = dec * ATT_HEADS
    steps = n_pages // pps

    def paged(r):
        return lambda b, s, pt: (pt[b, s * pps + r], 0, 0)

    per_seq = lambda b, s, pt: (b, 0, 0)
    in_specs = [pl.BlockSpec((1, dec, ATT_DIM), per_seq),
                pl.BlockSpec((1, ATT_DIM, page), per_seq),
                pl.BlockSpec((1, ATT_DIM, page), per_seq),
                pl.BlockSpec((1, ATT_HEADS, page), per_seq),
                pl.BlockSpec((1, ATT_DIM), lambda b, s, pt: (0, 0))]
    in_specs += [pl.BlockSpec((1, ATT_DIM, page), paged(r)) for r in range(pps)]
    in_specs += [pl.BlockSpec((1, ATT_DIM, page), paged(r)) for r in range(pps)]
    in_specs += [pl.BlockSpec((1, ATT_HEADS, page), paged(r)) for r in range(pps)]
    vmem = 2 * 2 * pps * page * ATT_DIM * 4 + 4 * page * ATT_DIM * 4 + 16 * rows_q * ATT_DIM * 4 + (8 << 20)
    grid_spec = pltpu.PrefetchScalarGridSpec(
        num_scalar_prefetch=1,
        grid=(nb, steps),
        in_specs=in_specs,
        out_specs=pl.BlockSpec((1, dec, ATT_DIM), per_seq),
        scratch_shapes=[pltpu.VMEM((rows_q, ATT_DIM), F32),
                        pltpu.VMEM((rows_q, 1), F32),
                        pltpu.VMEM((rows_q, 1), F32),
                        pltpu.VMEM((rows_q, ATT_DIM), F32),
                        pltpu.VMEM((ATT_HEADS, 1), F32)])
    return pl.pallas_call(
        functools.partial(_sattn_kernel, pps=pps, dec=dec),
        grid_spec=grid_spec,
        out_shape=jax.ShapeDtypeStruct((nb, dec, ATT_DIM), BF16),
        compiler_params=_params(("arbitrary", "arbitrary"), vmem),
        name="attn_sample",
    )(page_table, q_s, k_new, v_new, lf_new, g_att,
      *([cache_k] * pps), *([cache_v] * pps), *([cache_lft] * pps))


PAD_ROWS = 8


def _conv_kernel(*refs, tm, blocks_per_seq, period):
    if period is None:
        x_ref, g_ref, w_ref, cw_ref, gout_ref, o_ref, tail_ref, ubuf_ref = refs
    else:
        x_ref, g_ref, w_ref, cw_ref, gout_ref, f1_ref, f2_ref, o_ref, tail_ref, ubuf_ref = refs
    @pl.when(pl.program_id(0) % blocks_per_seq == 0)
    def _():
        ubuf_ref[0:PAD_ROWS, :] = jnp.zeros((PAD_ROWS, CONV_DIM), F32)

    x = x_ref[...]
    hb = ((x * _rms_scale(x)) * g_ref[...]).astype(BF16)
    u = _dot(hb, w_ref[:, CONV_DIM:2 * CONV_DIM]) * _dot(hb, w_ref[:, 0:CONV_DIM])
    ubuf_ref[PAD_ROWS:PAD_ROWS + tm, :] = u
    u1 = ubuf_ref[PAD_ROWS - 1:PAD_ROWS - 1 + tm, :]
    u2 = ubuf_ref[PAD_ROWS - 2:PAD_ROWS - 2 + tm, :]
    if period is not None:
        pos = lax.broadcasted_iota(jnp.int32, (tm, CONV_DIM), 0) % period
        u1 = jnp.where(pos < 1, f1_ref[...], u1)
        u2 = jnp.where(pos < 2, f2_ref[...], u2)
    conv = u2 * cw_ref[0:1, :] + u1 * cw_ref[1:2, :] + u * cw_ref[2:3, :]
    last = ubuf_ref[tm:tm + PAD_ROWS, :]
    ubuf_ref[0:PAD_ROWS, :] = last
    if period is None:
        tail_ref[0] = last
    else:
        tail_ref[...] = u

    y = _dot(hb, w_ref[:, 2 * CONV_DIM:3 * CONV_DIM]) * conv
    o_ref[...] = ((y * _rms_scale(y)) * gout_ref[...]).astype(BF16)


def _conv_proj(x, g, w_conv, conv_w, g_out, tm, seq, fills=None):
    m, d = x.shape
    row = lambda i: (i, 0)
    in_specs = [pl.BlockSpec((tm, d), row),
                _resident((1, d)),
                _resident((d, 3 * CONV_DIM)),
                _resident((CONV_W, CONV_DIM)),
                _resident((1, CONV_DIM))]
    args = [x, g, w_conv, conv_w, g_out]
    if fills is None:
        blocks_per_seq = seq // tm
        tail_spec = pl.BlockSpec((1, PAD_ROWS, CONV_DIM), lambda i: (i // blocks_per_seq, 0, 0))
        tail_shape = jax.ShapeDtypeStruct((m // seq, PAD_ROWS, CONV_DIM), F32)
        period = None
    else:
        blocks_per_seq = 1
        in_specs += [pl.BlockSpec((tm, CONV_DIM), row), pl.BlockSpec((tm, CONV_DIM), row)]
        args += list(fills)
        tail_spec = pl.BlockSpec((tm, CONV_DIM), row)
        tail_shape = jax.ShapeDtypeStruct((m, CONV_DIM), F32)
        period = seq
    vmem = (2 * tm * d * 4 + d * 3 * CONV_DIM * 2 + tm * d * 2 + 12 * tm * CONV_DIM * 4 + (8 << 20))
    return pl.pallas_call(
        functools.partial(_conv_kernel, tm=tm, blocks_per_seq=blocks_per_seq, period=period),
        grid=(m // tm,),
        in_specs=in_specs,
        out_specs=[pl.BlockSpec((tm, CONV_DIM), row), tail_spec],
        out_shape=[jax.ShapeDtypeStruct((m, CONV_DIM), BF16), tail_shape],
        scratch_shapes=[pltpu.VMEM((tm + PAD_ROWS, CONV_DIM), F32)],
        compiler_params=_params(("arbitrary",), vmem),
        name="conv_proj",
    )(*args)


def _out_kernel(x_ref, ma_ref, mc_ref, wa_ref, wc_ref, g_ref, x1_ref, h2_ref):
    x1 = x_ref[...] + (_dot(ma_ref[...], wa_ref[...]) + _dot(mc_ref[...], wc_ref[...]))
    x1_ref[...] = x1
    h2_ref[...] = ((x1 * _rms_scale(x1)) * g_ref[...]).astype(BF16)


def _out_proj(x, m_att, m_conv, w_a, w_c, g_mlp, tm):
    m, d = x.shape
    row = lambda i: (i, 0)
    const = lambda i: (0, 0)
    vmem = 2 * 2 * tm * d * 4 + 2 * tm * d * 2 + 4 * tm * ATT_DIM * 2 + 2 * 2 * ATT_DIM * d * 2 + (8 << 20)
    return pl.pallas_call(
        _out_kernel,
        grid=(m // tm,),
        in_specs=[pl.BlockSpec((tm, d), row),
                  pl.BlockSpec((tm, ATT_DIM), row),
                  pl.BlockSpec((tm, CONV_DIM), row),
                  pl.BlockSpec((ATT_DIM, d), const),
                  pl.BlockSpec((CONV_DIM, d), const),
                  pl.BlockSpec((1, d), const)],
        out_specs=[pl.BlockSpec((tm, d), row), pl.BlockSpec((tm, d), row)],
        out_shape=[jax.ShapeDtypeStruct((m, d), F32), jax.ShapeDtypeStruct((m, d), BF16)],
        compiler_params=_params(("arbitrary",), vmem),
        name="out_proj",
    )(x, m_att, m_conv, w_a, w_c, g_mlp)


def _mlp_kernel(h_ref, x1_ref, wu_ref, wd_ref, g_ref, y_ref, acc_ref):
    f = pl.program_id(1)

    @pl.when(f == 0)
    def _():
        acc_ref[...] = x1_ref[...]

    z = jnp.maximum(_dot(h_ref[...], wu_ref[...]), 0.0)
    acc_ref[...] += _dot((z * z).astype(BF16), wd_ref[...])

    @pl.when(f == pl.num_programs(1) - 1)
    def _():
        x2 = acc_ref[...]
        y_ref[...] = (x2 * _rms_scale(x2)) * g_ref[...]


def _mlp(h2, x1, w_up, w_down, g_final, tm, tf):
    m, d = x1.shape
    ff = w_up.shape[1]
    row = lambda i, f: (i, 0)
    vmem = (2 * tm * d * 2 + 2 * tm * d * 4 + 2 * 2 * d * tf * 2 + 2 * tm * d * 4 + tm * d * 4
            + 3 * tm * tf * 4 + (6 << 20))
    return pl.pallas_call(
        _mlp_kernel,
        grid=(m // tm, ff // tf),
        in_specs=[pl.BlockSpec((tm, d), row),
                  pl.BlockSpec((tm, d), row),
                  pl.BlockSpec((d, tf), lambda i, f: (0, f)),
                  pl.BlockSpec((tf, d), lambda i, f: (f, 0)),
                  pl.BlockSpec((1, d), lambda i, f: (0, 0))],
        out_specs=pl.BlockSpec((tm, d), row),
        out_shape=jax.ShapeDtypeStruct((m, d), F32),
        scratch_shapes=[pltpu.VMEM((tm, d), F32)],
        compiler_params=_params(("arbitrary", "arbitrary"), vmem),
        name="mlp",
    )(h2, x1, w_up, w_down, g_final)


def _keys_last(cache):
    n_pool, page = cache.shape[:2]
    return cache.transpose(0, 2, 3, 1).reshape(n_pool, ATT_DIM, page)


def _row_tile(m, want):
    return want if m % want == 0 else m


def _layer_weights(norm_mix, w_in, b_f, conv_w, norm_att_out, norm_conv_out, w_out, norm_mlp,
                   w_up, w_down):
    a = ATT_DIM
    c0 = 3 * a + ATT_HEADS
    w_qkv = w_in[:, :3 * a].astype(BF16)
    w_flt = w_in[:, 3 * a:c0].T.astype(BF16)
    w_conv = jnp.concatenate([w_in[:, c0:c0 + CONV_DIM],
                              w_in[:, c0 + 2 * CONV_DIM:c0 + 3 * CONV_DIM],
                              w_in[:, c0 + CONV_DIM:c0 + 2 * CONV_DIM]],
                             axis=1).astype(BF16)
    return dict(g_mix=norm_mix[None, :], w_qkv=w_qkv, w_flt=w_flt, b_f=b_f[:, None],
                w_conv=w_conv, conv_w=conv_w, g_att=norm_att_out[None, :],
                g_conv=norm_conv_out[None, :], w_a=w_out[:a].astype(BF16),
                w_c=w_out[a:].astype(BF16), g_mlp=norm_mlp[None, :],
                w_up=w_up.astype(BF16), w_down=w_down.astype(BF16))


def _tail(x, m_att, m_conv, w, g_final, tm, tf):
    x1, h2 = _out_proj(x, m_att, m_conv, w["w_a"], w["w_c"], w["g_mlp"], tm)
    return _mlp(h2, x1, w["w_up"], w["w_down"], g_final, tm, _row_tile(w["w_up"].shape[1], tf))


def kernel(x_prompt, x_sample, cache_k, cache_v, cache_logf, state_conv, page_table, norm_mix, w_in,
           b_f, conv_w, norm_att_out, norm_conv_out, w_out, norm_mlp, w_up, w_down, norm_final):
    depth = w_in.shape[0]
    assert depth == 1, "single-layer step"
    batch, seq, d = x_prompt.shape
    nb, dec, _ = x_sample.shape
    page = cache_k.shape[2]
    w = _layer_weights(norm_mix[0], w_in[0], b_f[0], conv_w[0], norm_att_out[0], norm_conv_out[0],
                       w_out[0], norm_mlp[0], w_up[0], w_down[0])
    g_final = norm_final[None, :]

    xp = x_prompt.reshape(batch * seq, d)
    pairs_p, k_p, v_p, lft_p = _qkv_proj(xp, w["g_mix"], w["w_qkv"], w["w_flt"], w["b_f"],
                                         _row_tile(seq, 256))
    tq = _row_tile(seq, 256)
    m_att = _attn_prompt(pairs_p, lft_p, w["g_att"], batch, seq, tq, tq)
    m_conv, tail_p = _conv_proj(xp, w["g_mix"], w["w_conv"], w["conv_w"], w["g_conv"],
                                _row_tile(seq, 512), seq)
    y_p = _tail(xp, m_att, m_conv, w, g_final, _row_tile(seq, 512), 1024)

    ms = nb * dec
    xs = x_sample.reshape(ms, d)
    pairs_s, k_s, v_s, lft_s = _qkv_proj(xs, w["g_mix"], w["w_qkv"], w["w_flt"], w["b_f"], ms)
    q_s = pairs_s[0].transpose(1, 0, 2).reshape(nb, dec, ATT_DIM)
    pad = ((0, 0), (0, 0), (0, page - dec))
    k_new = jnp.pad(k_s.reshape(nb, dec, ATT_DIM).transpose(0, 2, 1), pad)
    v_new = jnp.pad(v_s.reshape(nb, dec, ATT_DIM).transpose(0, 2, 1), pad)
    lf_new = jnp.pad(lft_s.reshape(ATT_HEADS, nb, dec).transpose(1, 0, 2), pad)
    n_pool = cache_k.shape[1]
    m_att_s = _attn_sample(page_table, q_s, k_new, v_new, lf_new, w["g_att"],
                           _keys_last(cache_k[0]), _keys_last(cache_v[0]),
                           cache_logf[0].transpose(0, 2, 1), SAMPLE_PAGES_PER_STEP)
    st = state_conv[0]
    zeros = jnp.zeros((nb, dec - 1, CONV_DIM), F32)
    f1 = jnp.concatenate([st[:, 1:2], zeros], axis=1).reshape(ms, CONV_DIM)
    f2 = jnp.concatenate([st, zeros[:, 1:]], axis=1).reshape(ms, CONV_DIM)
    m_conv_s, u_s = _conv_proj(xs, w["g_mix"], w["w_conv"], w["conv_w"], w["g_conv"], ms, dec,
                               fills=(f1, f2))
    y_s = _tail(xs, m_att_s.reshape(ms, ATT_DIM), m_conv_s, w, g_final, ms, 1024)

    hd = (ATT_HEADS, HEAD_DIM)
    return (y_p.reshape(batch, seq, d),
            y_s.reshape(nb, dec, d),
            k_p.reshape(1, batch, seq, *hd),
            v_p.reshape(1, batch, seq, *hd),
            lft_p.T.reshape(1, batch, seq, ATT_HEADS),
            tail_p[:, PAD_ROWS - (CONV_W - 1):][None],
            k_s.reshape(1, nb, dec, *hd),
            v_s.reshape(1, nb, dec, *hd),
            lft_s.T.reshape(1, nb, dec, ATT_HEADS),
            u_s.reshape(nb, dec, CONV_DIM)[:, dec - (CONV_W - 1):][None])
```

```python
import functools

import jax
import jax.numpy as jnp
from jax import lax
from jax.experimental import pallas as pl
from jax.experimental.pallas import tpu as pltpu

F32 = jnp.float32
BF16 = jnp.bfloat16

ATT_HEADS = 16
HEAD_DIM = 64
ATT_DIM = ATT_HEADS * HEAD_DIM
CONV_DIM = 1024
CONV_W = 3
EPS = 1e-6
LANES = 128
HEAD_PAIRS = ATT_DIM // LANES
Q_SCALE = HEAD_DIM ** -0.5
NEG = -1e30
V7X_VMEM_LIMIT = 56 * 1024 * 1024


def _params(semantics, vmem_bytes):
    return pltpu.CompilerParams(dimension_semantics=semantics,
                                vmem_limit_bytes=min(int(vmem_bytes), V7X_VMEM_LIMIT))


def _dot(a, b):
    return jnp.dot(a, b, preferred_element_type=F32)


def _dot_nt(a, b):
    return lax.dot_general(a, b, (((1,), (1,)), ((), ())), preferred_element_type=F32)


def _split_bf16(x):
    hi = x.astype(BF16)
    lo = (x - hi.astype(F32)).astype(BF16)
    return hi, lo


def _upper_ones(n):
    r = lax.broadcasted_iota(jnp.int32, (n, n), 0)
    c = lax.broadcasted_iota(jnp.int32, (n, n), 1)
    return (r <= c).astype(BF16)


def _prefix_lanes(x, upper):
    hi, lo = _split_bf16(x)
    return _dot(hi, upper) + _dot(lo, upper)


def _rms_scale(x):
    return lax.rsqrt(jnp.mean(x * x, axis=-1, keepdims=True) + EPS)


def _log_sigmoid(z):
    return jnp.minimum(z, 0.0) - jnp.log1p(jnp.exp(-jnp.abs(z)))


PAD_ROWS = 8


def _resident(shape):
    return pl.BlockSpec(shape, lambda i: (0,) * len(shape), pipeline_mode=pl.Buffered(1))


def _in_proj_kernel(*refs, tm, blocks_per_seq, period):
    if period is None:
        (x0_ref, xn_ref, g_ref, w_ref, wflt_ref, bf_ref, cw_ref, gout_ref,
         pairs_ref, k_ref, v_ref, lft_ref, mconv_ref, tail_ref, h_ref, ubuf_ref) = refs
    else:
        (x0_ref, xn_ref, g_ref, w_ref, wflt_ref, bf_ref, cw_ref, gout_ref, f1_ref, f2_ref,
         pairs_ref, k_ref, v_ref, lft_ref, mconv_ref, tail_ref, h_ref, ubuf_ref) = refs
    i = pl.program_id(0)

    def normed(x):
        return ((x * _rms_scale(x)) * g_ref[...]).astype(BF16)

    @pl.when(i == 0)
    def _():
        h_ref[0] = normed(x0_ref[...])

    @pl.when(i % blocks_per_seq == 0)
    def _():
        ubuf_ref[0:PAD_ROWS, :] = jnp.zeros((PAD_ROWS, CONV_DIM), F32)

    slot = i % 2
    hb = h_ref[slot]
    h_ref[1 - slot] = normed(xn_ref[...])

    def proj(j):
        return _dot(hb, w_ref[:, j * ATT_DIM:(j + 1) * ATT_DIM])

    lft_ref[...] = _log_sigmoid(_dot_nt(wflt_ref[...], hb) + bf_ref[...])
    for j, heads_ref in enumerate((None, k_ref, v_ref)):
        r = proj(j)
        scaled = r * Q_SCALE if j == 0 else r
        for hp in range(HEAD_PAIRS):
            pairs_ref[j, hp] = scaled[:, hp * LANES:(hp + 1) * LANES].astype(BF16)
        if heads_ref is not None:
            for h in range(ATT_HEADS):
                heads_ref[pl.ds(h, tm, stride=ATT_HEADS), :] = r[:, h * HEAD_DIM:(h + 1) * HEAD_DIM]

    u = proj(4) * proj(3)
    ubuf_ref[PAD_ROWS:PAD_ROWS + tm, :] = u
    u1 = ubuf_ref[PAD_ROWS - 1:PAD_ROWS - 1 + tm, :]
    u2 = ubuf_ref[PAD_ROWS - 2:PAD_ROWS - 2 + tm, :]
    if period is not None:
        pos = lax.broadcasted_iota(jnp.int32, (tm, CONV_DIM), 0) % period
        u1 = jnp.where(pos < 1, f1_ref[...], u1)
        u2 = jnp.where(pos < 2, f2_ref[...], u2)
    conv = u2 * cw_ref[0:1, :] + u1 * cw_ref[1:2, :] + u * cw_ref[2:3, :]
    last = ubuf_ref[tm:tm + PAD_ROWS, :]
    ubuf_ref[0:PAD_ROWS, :] = last
    if period is None:
        tail_ref[0] = last
    else:
        tail_ref[...] = u
    y = proj(5) * conv
    mconv_ref[...] = ((y * _rms_scale(y)) * gout_ref[...]).astype(BF16)


def _in_proj(x, w, tm, seq, fills=None):
    m, d = x.shape
    n = m // tm
    row = lambda i: (i, 0)
    in_specs = [pl.BlockSpec((tm, d), lambda i: (0, 0), pipeline_mode=pl.Buffered(1)),
                pl.BlockSpec((tm, d), lambda i: (jnp.minimum(i + 1, n - 1), 0)),
                _resident((1, d)),
                _resident((d, 6 * ATT_DIM)),
                _resident((ATT_HEADS, d)),
                _resident((ATT_HEADS, 1)),
                _resident((CONV_W, CONV_DIM)),
                _resident((1, CONV_DIM))]
    args = [x, x, w["g_mix"], w["w_all"], w["w_flt"], w["b_f"], w["conv_w"], w["g_conv"]]
    if fills is None:
        blocks_per_seq = seq // tm
        tail_spec = pl.BlockSpec((1, PAD_ROWS, CONV_DIM), lambda i: (i // blocks_per_seq, 0, 0))
        tail_shape = jax.ShapeDtypeStruct((m // seq, PAD_ROWS, CONV_DIM), F32)
        period = None
    else:
        blocks_per_seq = 1
        in_specs += [pl.BlockSpec((tm, CONV_DIM), row), pl.BlockSpec((tm, CONV_DIM), row)]
        args += list(fills)
        tail_spec = pl.BlockSpec((tm, CONV_DIM), row)
        tail_shape = jax.ShapeDtypeStruct((m, CONV_DIM), F32)
        period = seq
    heads_rows = tm * ATT_HEADS * LANES * 4
    vmem = (3 * tm * d * 4 + d * 6 * ATT_DIM * 2 + 2 * 3 * tm * ATT_DIM * 2 + 2 * 2 * heads_rows
            + 2 * tm * d * 2 + 12 * tm * ATT_DIM * 4 + (6 << 20))
    heads_spec = pl.BlockSpec((tm * ATT_HEADS, HEAD_DIM), row)
    heads_shape = jax.ShapeDtypeStruct((m * ATT_HEADS, HEAD_DIM), F32)
    return pl.pallas_call(
        functools.partial(_in_proj_kernel, tm=tm, blocks_per_seq=blocks_per_seq, period=period),
        grid=(n,),
        in_specs=in_specs,
        out_specs=[pl.BlockSpec((3, HEAD_PAIRS, tm, LANES), lambda i: (0, 0, i, 0)),
                   heads_spec, heads_spec,
                   pl.BlockSpec((ATT_HEADS, tm), lambda i: (0, i)),
                   pl.BlockSpec((tm, CONV_DIM), row),
                   tail_spec],
        out_shape=[jax.ShapeDtypeStruct((3, HEAD_PAIRS, m, LANES), BF16),
                   heads_shape, heads_shape,
                   jax.ShapeDtypeStruct((ATT_HEADS, m), F32),
                   jax.ShapeDtypeStruct((m, CONV_DIM), BF16),
                   tail_shape],
        scratch_shapes=[pltpu.VMEM((2, tm, d), BF16),
                        pltpu.VMEM((tm + PAD_ROWS, CONV_DIM), F32)],
        compiler_params=_params(("arbitrary",), vmem),
        name="in_proj",
    )(*args)


def _attn_kernel(q_ref, k_ref, v_ref, lft_ref, g_ref, o_ref,
                 negc_ref, qx_ref, vx_ref, m_ref, acc_ref, *, tq, tk, seq):
    qi = pl.program_id(1)
    n_kv = seq // tk
    diag = tq // tk
    n_chunk = tk // LANES

    @pl.when(qi == 0)
    def _():
        upper = _upper_ones(tk)
        carry = jnp.zeros((ATT_HEADS, 1), F32)
        for kb in range(n_kv):
            c_blk = _prefix_lanes(lft_ref[:, kb * tk:(kb + 1) * tk], upper) + carry
            negc_ref[kb * ATT_HEADS:(kb + 1) * ATT_HEADS, :] = -c_blk
            carry = c_blk[:, tk - 1:tk]

        lane_k = lax.broadcasted_iota(jnp.int32, (tk, LANES), 1)

        def indicator(mask):
            return jnp.where(mask, 1.0, 0.0).astype(BF16)

        keep = (indicator(lane_k < HEAD_DIM), indicator(lane_k >= HEAD_DIM))

        def build(kb, _):
            rows_k = pl.ds(pl.multiple_of(kb * tk, tk), tk)
            for hp in range(HEAD_PAIRS):
                v2 = v_ref[hp, rows_k, :]
                for a in range(2):
                    vx_ref[2 * hp + a, rows_k, :] = v2 * keep[a] + keep[1 - a]
            return 0

        lax.fori_loop(0, n_kv, build, 0)

    is_a = lax.broadcasted_iota(jnp.int32, (tq, LANES), 1) < HEAD_DIM
    for hp in range(HEAD_PAIRS):
        q2 = q_ref[hp]
        zero = jnp.zeros_like(q2)
        qx_ref[hp, 0:tq, :] = jnp.where(is_a, q2, zero)
        qx_ref[hp, tq:2 * tq, :] = jnp.where(is_a, zero, q2)
    m_ref[...] = jnp.full(m_ref.shape, NEG, F32)
    acc_ref[...] = jnp.zeros(acc_ref.shape, F32)

    row = lax.broadcasted_iota(jnp.int32, (tq, LANES), 0)
    col = lax.broadcasted_iota(jnp.int32, (tq, LANES), 1)

    def kv_step(kv, masked):
        rows_k = pl.ds(pl.multiple_of(kv * tk, tk), tk)
        col_shift = kv * tk - qi * tq
        for hp in range(HEAD_PAIRS):
            s2 = _dot_nt(qx_ref[hp], k_ref[hp, rows_k, :])
            for a in range(2):
                h = 2 * hp + a
                bias = negc_ref[pl.ds(kv * ATT_HEADS + h, 1), :]
                chunks = []
                for c in range(n_chunk):
                    lanes = slice(c * LANES, (c + 1) * LANES)
                    sc = s2[a * tq:(a + 1) * tq, lanes] + bias[:, lanes]
                    if masked:
                        sc = jnp.where(col + (col_shift + c * LANES) <= row, sc, NEG)
                    chunks.append(sc)
                mx = chunks[0]
                for sc in chunks[1:]:
                    mx = jnp.maximum(mx, sc)
                m_old = m_ref[h]
                m_new = jnp.maximum(m_old, jnp.max(mx, axis=-1, keepdims=True))
                alpha = jnp.exp(m_old - m_new)
                p = jnp.concatenate([jnp.exp(sc - m_new) for sc in chunks], axis=1).astype(BF16)
                acc_ref[h] = alpha * acc_ref[h] + _dot(p, vx_ref[h, rows_k, :])
                m_ref[h] = m_new

    def full_step(kv, _):
        kv_step(kv, False)
        return 0

    def diag_step(kv, _):
        kv_step(kv, True)
        return 0

    lax.fori_loop(0, qi * diag, full_step, 0)
    lax.fori_loop(qi * diag, (qi + 1) * diag, diag_step, 0)

    outs = []
    ss = jnp.zeros((tq, LANES), F32)
    for hp in range(HEAD_PAIRS):
        acc_a, acc_b = acc_ref[2 * hp], acc_ref[2 * hp + 1]
        den = pltpu.roll(jnp.where(is_a, acc_b, acc_a), HEAD_DIM, axis=1)
        o = jnp.where(is_a, acc_a, acc_b) / den
        outs.append(o)
        ss = ss + o * o
    inv = lax.rsqrt(jnp.sum(ss, axis=-1, keepdims=True) * (1.0 / ATT_DIM) + EPS)
    for hp in range(HEAD_PAIRS):
        sl = slice(hp * LANES, (hp + 1) * LANES)
        o_ref[:, sl] = ((outs[hp] * inv) * g_ref[:, sl]).astype(BF16)


def _attn_prompt(qkv_pairs, lft, g_att, batch, seq, tq, tk):
    m = batch * seq
    nq = seq // tq
    vmem = (2 * 2 * seq * ATT_DIM * 2 + seq * ATT_DIM * 2 * 2 + 2 * ATT_HEADS * tq * LANES * 4
            + 4 * tq * ATT_DIM * 2 + (12 << 20))
    return pl.pallas_call(
        functools.partial(_attn_kernel, tq=tq, tk=tk, seq=seq),
        grid=(batch, nq),
        in_specs=[pl.BlockSpec((None, HEAD_PAIRS, tq, LANES), lambda b, i: (0, 0, b * nq + i, 0)),
                  pl.BlockSpec((None, HEAD_PAIRS, seq, LANES), lambda b, i: (1, 0, b, 0)),
                  pl.BlockSpec((None, HEAD_PAIRS, seq, LANES), lambda b, i: (2, 0, b, 0)),
                  pl.BlockSpec((ATT_HEADS, seq), lambda b, i: (0, b)),
                  pl.BlockSpec((1, ATT_DIM), lambda b, i: (0, 0))],
        out_specs=pl.BlockSpec((tq, ATT_DIM), lambda b, i: (b * nq + i, 0)),
        out_shape=jax.ShapeDtypeStruct((m, ATT_DIM), BF16),
        scratch_shapes=[pltpu.VMEM((seq // tk * ATT_HEADS, tk), F32),
                        pltpu.VMEM((HEAD_PAIRS, 2 * tq, LANES), BF16),
                        pltpu.VMEM((ATT_HEADS, seq, LANES), BF16),
                        pltpu.VMEM((ATT_HEADS, tq, LANES), F32),
                        pltpu.VMEM((ATT_HEADS, tq, LANES), F32)],
        compiler_params=_params(("arbitrary", "arbitrary"), vmem),
        name="attn_prompt",
    )(qkv_pairs, qkv_pairs, qkv_pairs, lft, g_att)


def _sattn_kernel(pt_ref, q_ref, kn_ref, vn_ref, lfn_ref, g_ref, *refs, pps, dec):
    del pt_ref
    k_refs, v_refs, lf_refs = refs[:pps], refs[pps:2 * pps], refs[2 * pps:3 * pps]
    o_ref, qbd_ref, m_ref, l_ref, acc_ref, carry_ref = refs[3 * pps:]
    step = pl.program_id(1)
    rows_q = dec * ATT_HEADS
    page = k_refs[0].shape[2]

    @pl.when(step == 0)
    def _():
        q = q_ref[0].astype(F32)
        head_of_col = lax.broadcasted_iota(jnp.int32, (ATT_HEADS, ATT_DIM), 1) // HEAD_DIM
        head_of_row = lax.broadcasted_iota(jnp.int32, (ATT_HEADS, ATT_DIM), 0)
        own = head_of_col == head_of_row
        for t in range(dec):
            qt = jnp.broadcast_to(q[t:t + 1, :], (ATT_HEADS, ATT_DIM))
            qbd_ref[t * ATT_HEADS:(t + 1) * ATT_HEADS, :] = jnp.where(own, qt, 0.0)
        m_ref[...] = jnp.full(m_ref.shape, NEG, F32)
        l_ref[...] = jnp.zeros(l_ref.shape, F32)
        acc_ref[...] = jnp.zeros(acc_ref.shape, F32)
        carry_ref[...] = jnp.zeros(carry_ref.shape, F32)

    upper = _upper_ones(page)

    def update(scores, values):
        m_old = m_ref[...]
        m_new = m_old
        for s in scores:
            m_new = jnp.maximum(m_new, jnp.max(s, axis=-1, keepdims=True))
        alpha = jnp.exp(m_old - m_new)
        l_new = alpha * l_ref[...]
        pv = None
        for s, v in zip(scores, values):
            p = jnp.exp(s - m_new)
            l_new = l_new + jnp.sum(p, axis=-1, keepdims=True)
            d = _dot_nt(p, v)
            pv = d if pv is None else pv + d
        acc_ref[...] = alpha * acc_ref[...] + pv
        l_ref[...] = l_new
        m_ref[...] = m_new

    def scores_of(k_page, c_page):
        return _dot(qbd_ref[...], k_page) - jnp.tile(c_page, (dec, 1))

    lf_all = jnp.concatenate([lf_refs[r][0] for r in range(pps)], axis=0)
    within = _prefix_lanes(lf_all, upper)
    car = carry_ref[...]
    scores = []
    for r in range(pps):
        blk = within[r * ATT_HEADS:(r + 1) * ATT_HEADS, :]
        scores.append(scores_of(k_refs[r][0], blk + car))
        car = car + blk[:, page - 1:page]
    carry_ref[...] = car
    update(scores, [v_refs[r][0] for r in range(pps)])

    @pl.when(step == pl.num_programs(1) - 1)
    def _():
        c_new = _prefix_lanes(lfn_ref[0], upper) + carry_ref[...]
        s = scores_of(kn_ref[0], c_new)
        t_of_row = lax.broadcasted_iota(jnp.int32, (rows_q, page), 0) // ATT_HEADS
        key = lax.broadcasted_iota(jnp.int32, (rows_q, page), 1)
        update([jnp.where(key <= t_of_row, s, NEG)], [vn_ref[0]])
        o = acc_ref[...] / l_ref[...]
        head_of_col = lax.broadcasted_iota(jnp.int32, (rows_q, ATT_DIM), 1) // HEAD_DIM
        head_of_row = lax.broadcasted_iota(jnp.int32, (rows_q, ATT_DIM), 0) % ATT_HEADS
        o = jnp.where(head_of_col == head_of_row, o, 0.0)
        y = jnp.sum(o.reshape(dec, ATT_HEADS, ATT_DIM), axis=1)
        o_ref[0] = ((y * _rms_scale(y)) * g_ref[...]).astype(BF16)


SAMPLE_PAGES_PER_STEP = 16


def _attn_sample(page_table, q_s, k_new, v_new, lf_new, g_att, cache_k, cache_v, cache_lft, pps):
    nb, n_pages = page_table.shape
    pps = pps if n_pages % pps == 0 else n_pages
    dec = q_s.shape[1]
    page = cache_k.shape[2]
    rows_q = dec * ATT_HEADS
    steps = n_pages // pps

    def paged(r):
        return lambda b, s, pt: (pt[b, s * pps + r], 0, 0)

    per_seq = lambda b, s, pt: (b, 0, 0)
    in_specs = [pl.BlockSpec((1, dec, ATT_DIM), per_seq),
                pl.BlockSpec((1, ATT_DIM, page), per_seq),
                pl.BlockSpec((1, ATT_DIM, page), per_seq),
                pl.BlockSpec((1, ATT_HEADS, page), per_seq),
                pl.BlockSpec((1, ATT_DIM), lambda b, s, pt: (0, 0))]
    in_specs += [pl.BlockSpec((1, ATT_DIM, page), paged(r)) for r in range(pps)]
    in_specs += [pl.BlockSpec((1, ATT_DIM, page), paged(r)) for r in range(pps)]
    in_specs += [pl.BlockSpec((1, ATT_HEADS, page), paged(r)) for r in range(pps)]
    vmem = 2 * 2 * pps * page * ATT_DIM * 4 + 4 * page * ATT_DIM * 4 + 16 * rows_q * ATT_DIM * 4 + (8 << 20)
    grid_spec = pltpu.PrefetchScalarGridSpec(
        num_scalar_prefetch=1,
        grid=(nb, steps),
        in_specs=in_specs,
        out_specs=pl.BlockSpec((1, dec, ATT_DIM), per_seq),
        scratch_shapes=[pltpu.VMEM((rows_q, ATT_DIM), F32),
                        pltpu.VMEM((rows_q, 1), F32),
                        pltpu.VMEM((rows_q, 1), F32),
                        pltpu.VMEM((rows_q, ATT_DIM), F32),
                        pltpu.VMEM((ATT_HEADS, 1), F32)])
    return pl.pallas_call(
        functools.partial(_sattn_kernel, pps=pps, dec=dec),
        grid_spec=grid_spec,
        out_shape=jax.ShapeDtypeStruct((nb, dec, ATT_DIM), BF16),
        compiler_params=_params(("arbitrary", "arbitrary"), vmem),
        name="attn_sample",
    )(page_table, q_s, k_new, v_new, lf_new, g_att,
      *([cache_k] * pps), *([cache_v] * pps), *([cache_lft] * pps))


def _out_kernel(x_ref, ma_ref, mc_ref, wa_ref, wc_ref, g_ref, x1_ref, h2_ref):
    x1 = x_ref[...] + (_dot(ma_ref[...], wa_ref[...]) + _dot(mc_ref[...], wc_ref[...]))
    x1_ref[...] = x1
    h2_ref[...] = ((x1 * _rms_scale(x1)) * g_ref[...]).astype(BF16)


def _out_proj(x, m_att, m_conv, w_a, w_c, g_mlp, tm):
    m, d = x.shape
    row = lambda i: (i, 0)
    const = lambda i: (0, 0)
    vmem = 2 * 2 * tm * d * 4 + 2 * tm * d * 2 + 4 * tm * ATT_DIM * 2 + 2 * 2 * ATT_DIM * d * 2 + (8 << 20)
    return pl.pallas_call(
        _out_kernel,
        grid=(m // tm,),
        in_specs=[pl.BlockSpec((tm, d), row),
                  pl.BlockSpec((tm, ATT_DIM), row),
                  pl.BlockSpec((tm, CONV_DIM), row),
                  pl.BlockSpec((ATT_DIM, d), const),
                  pl.BlockSpec((CONV_DIM, d), const),
                  pl.BlockSpec((1, d), const)],
        out_specs=[pl.BlockSpec((tm, d), row), pl.BlockSpec((tm, d), row)],
        out_shape=[jax.ShapeDtypeStruct((m, d), F32), jax.ShapeDtypeStruct((m, d), BF16)],
        compiler_params=_params(("arbitrary",), vmem),
        name="out_proj",
    )(x, m_att, m_conv, w_a, w_c, g_mlp)


def _mlp_kernel(h_ref, x1_ref, wu_ref, wd_ref, g_ref, y_ref, acc_ref):
    f = pl.program_id(1)

    @pl.when(f == 0)
    def _():
        acc_ref[...] = x1_ref[...]

    z = jnp.maximum(_dot(h_ref[...], wu_ref[...]), 0.0)
    acc_ref[...] += _dot((z * z).astype(BF16), wd_ref[...])

    @pl.when(f == pl.num_programs(1) - 1)
    def _():
        x2 = acc_ref[...]
        y_ref[...] = (x2 * _rms_scale(x2)) * g_ref[...]


def _mlp(h2, x1, w_up, w_down, g_final, tm, tf):
    m, d = x1.shape
    ff = w_up.shape[1]
    row = lambda i, f: (i, 0)
    vmem = (2 * tm * d * 2 + 2 * tm * d * 4 + 2 * 2 * d * tf * 2 + 2 * tm * d * 4 + tm * d * 4
            + 3 * tm * tf * 4 + (6 << 20))
    return pl.pallas_call(
        _mlp_kernel,
        grid=(m // tm, ff // tf),
        in_specs=[pl.BlockSpec((tm, d), row),
                  pl.BlockSpec((tm, d), row),
                  pl.BlockSpec((d, tf), lambda i, f: (0, f)),
                  pl.BlockSpec((tf, d), lambda i, f: (f, 0)),
                  pl.BlockSpec((1, d), lambda i, f: (0, 0))],
        out_specs=pl.BlockSpec((tm, d), row),
        out_shape=jax.ShapeDtypeStruct((m, d), F32),
        scratch_shapes=[pltpu.VMEM((tm, d), F32)],
        compiler_params=_params(("arbitrary", "arbitrary"), vmem),
        name="mlp",
    )(h2, x1, w_up, w_down, g_final)


def _keys_last(cache):
    n_pool, page = cache.shape[:2]
    return cache.transpose(0, 2, 3, 1).reshape(n_pool, ATT_DIM, page)


def _row_tile(m, want):
    return want if m % want == 0 else m


def _layer_weights(norm_mix, w_in, b_f, conv_w, norm_att_out, norm_conv_out, w_out, norm_mlp,
                   w_up, w_down):
    a = ATT_DIM
    c0 = 3 * a + ATT_HEADS
    w_qkv = w_in[:, :3 * a].astype(BF16)
    w_flt = w_in[:, 3 * a:c0].T.astype(BF16)
    w_conv = jnp.concatenate([w_in[:, c0:c0 + CONV_DIM],
                              w_in[:, c0 + 2 * CONV_DIM:c0 + 3 * CONV_DIM],
                              w_in[:, c0 + CONV_DIM:c0 + 2 * CONV_DIM]],
                             axis=1).astype(BF16)
    return dict(g_mix=norm_mix[None, :], w_all=jnp.concatenate([w_qkv, w_conv], axis=1),
                w_flt=w_flt, b_f=b_f[:, None], conv_w=conv_w, g_att=norm_att_out[None, :],
                g_conv=norm_conv_out[None, :], w_a=w_out[:a].astype(BF16),
                w_c=w_out[a:].astype(BF16), g_mlp=norm_mlp[None, :],
                w_up=w_up.astype(BF16), w_down=w_down.astype(BF16))


def _tail(x, m_att, m_conv, w, g_final, tm, tf):
    x1, h2 = _out_proj(x, m_att, m_conv, w["w_a"], w["w_c"], w["g_mlp"], tm)
    return _mlp(h2, x1, w["w_up"], w["w_down"], g_final, tm, _row_tile(w["w_up"].shape[1], tf))


def kernel(x_prompt, x_sample, cache_k, cache_v, cache_logf, state_conv, page_table, norm_mix, w_in,
           b_f, conv_w, norm_att_out, norm_conv_out, w_out, norm_mlp, w_up, w_down, norm_final):
    depth = w_in.shape[0]
    assert depth == 1, "single-layer step"
    batch, seq, d = x_prompt.shape
    nb, dec, _ = x_sample.shape
    page = cache_k.shape[2]
    w = _layer_weights(norm_mix[0], w_in[0], b_f[0], conv_w[0], norm_att_out[0], norm_conv_out[0],
                       w_out[0], norm_mlp[0], w_up[0], w_down[0])
    g_final = norm_final[None, :]

    xp = x_prompt.reshape(batch * seq, d)
    pairs_p, k_p, v_p, lft_p, m_conv, tail_p = _in_proj(xp, w, _row_tile(seq, 256), seq)
    tq = _row_tile(seq, 256)
    m_att = _attn_prompt(pairs_p, lft_p, w["g_att"], batch, seq, tq, tq)
    y_p = _tail(xp, m_att, m_conv, w, g_final, _row_tile(seq, 512), 1024)

    ms = nb * dec
    xs = x_sample.reshape(ms, d)
    st = state_conv[0]
    zeros = jnp.zeros((nb, dec - 1, CONV_DIM), F32)
    f1 = jnp.concatenate([st[:, 1:2], zeros], axis=1).reshape(ms, CONV_DIM)
    f2 = jnp.concatenate([st, zeros[:, 1:]], axis=1).reshape(ms, CONV_DIM)
    pairs_s, k_s, v_s, lft_s, m_conv_s, u_s = _in_proj(xs, w, ms, dec, fills=(f1, f2))
    q_s = pairs_s[0].transpose(1, 0, 2).reshape(nb, dec, ATT_DIM)
    pad = ((0, 0), (0, 0), (0, page - dec))
    k_new = jnp.pad(k_s.reshape(nb, dec, ATT_DIM).transpose(0, 2, 1), pad)
    v_new = jnp.pad(v_s.reshape(nb, dec, ATT_DIM).transpose(0, 2, 1), pad)
    lf_new = jnp.pad(lft_s.reshape(ATT_HEADS, nb, dec).transpose(1, 0, 2), pad)
    n_pool = cache_k.shape[1]
    m_att_s = _attn_sample(page_table, q_s, k_new, v_new, lf_new, w["g_att"],
                           _keys_last(cache_k[0]), _keys_last(cache_v[0]),
                           cache_logf[0].transpose(0, 2, 1), SAMPLE_PAGES_PER_STEP)
    y_s = _tail(xs, m_att_s.reshape(ms, ATT_DIM), m_conv_s, w, g_final, ms, 1024)

    hd = (ATT_HEADS, HEAD_DIM)
    return (y_p.reshape(batch, seq, d),
            y_s.reshape(nb, dec, d),
            k_p.reshape(1, batch, seq, *hd),
            v_p.reshape(1, batch, seq, *hd),
            lft_p.T.reshape(1, batch, seq, ATT_HEADS),
            tail_p[:, PAD_ROWS - (CONV_W - 1):][None],
            k_s.reshape(1, nb, dec, *hd),
            v_s.reshape(1, nb, dec, *hd),
            lft_s.T.reshape(1, nb, dec, ATT_HEADS),
            u_s.reshape(nb, dec, CONV_DIM)[:, dec - (CONV_W - 1):][None])
```

```python
import functools

import jax
import jax.numpy as jnp
from jax import lax
from jax.experimental import pallas as pl
from jax.experimental.pallas import tpu as pltpu

F32 = jnp.float32
BF16 = jnp.bfloat16

ATT_HEADS = 16
HEAD_DIM = 64
ATT_DIM = ATT_HEADS * HEAD_DIM
CONV_DIM = 1024
CONV_W = 3
EPS = 1e-6
LANES = 128
HEAD_PAIRS = ATT_DIM // LANES
LOG2E = 1.4426950408889634
Q_SCALE = HEAD_DIM ** -0.5 * LOG2E
NEG = -1e30
V7X_VMEM_LIMIT = 56 * 1024 * 1024


def _params(semantics, vmem_bytes):
    return pltpu.CompilerParams(dimension_semantics=semantics,
                                vmem_limit_bytes=min(int(vmem_bytes), V7X_VMEM_LIMIT))


def _dot(a, b):
    return jnp.dot(a, b, preferred_element_type=F32)


def _dot_nt(a, b):
    return lax.dot_general(a, b, (((1,), (1,)), ((), ())), preferred_element_type=F32)


def _split_bf16(x):
    hi = x.astype(BF16)
    lo = (x - hi.astype(F32)).astype(BF16)
    return hi, lo


def _upper_ones(n):
    r = lax.broadcasted_iota(jnp.int32, (n, n), 0)
    c = lax.broadcasted_iota(jnp.int32, (n, n), 1)
    return (r <= c).astype(BF16)


def _prefix_lanes(x, upper):
    hi, lo = _split_bf16(x)
    return _dot(hi, upper) + _dot(lo, upper)


def _rms_scale(x):
    return lax.rsqrt(jnp.mean(x * x, axis=-1, keepdims=True) + EPS)


def _log_sigmoid(z):
    return jnp.minimum(z, 0.0) - jnp.log1p(jnp.exp(-jnp.abs(z)))


PAD_ROWS = 8


def _resident(shape):
    return pl.BlockSpec(shape, lambda i: (0,) * len(shape), pipeline_mode=pl.Buffered(1))


def _in_proj_kernel(*refs, tm, blocks_per_seq, period):
    if period is None:
        (x0_ref, xn_ref, g_ref, wa_ref, wc_ref, wflt_ref, bf_ref, cw_ref, gout_ref,
         pairs_ref, k_ref, v_ref, lft_ref, mconv_ref, tail_ref, h_ref, ubuf_ref) = refs
    else:
        (x0_ref, xn_ref, g_ref, wa_ref, wc_ref, wflt_ref, bf_ref, cw_ref, gout_ref, f1_ref, f2_ref,
         pairs_ref, k_ref, v_ref, lft_ref, mconv_ref, tail_ref, h_ref, ubuf_ref) = refs
    i = pl.program_id(0)

    def normed(x):
        return ((x * _rms_scale(x)) * g_ref[...]).astype(BF16)

    @pl.when(i == 0)
    def _():
        h_ref[0] = normed(x0_ref[...])

    @pl.when(i % blocks_per_seq == 0)
    def _():
        ubuf_ref[0:PAD_ROWS, :] = jnp.zeros((PAD_ROWS, CONV_DIM), F32)

    slot = i % 2
    hb = h_ref[slot]
    h_ref[1 - slot] = normed(xn_ref[...])

    def proj(j):
        w_ref, c = (wa_ref, j) if j < 3 else (wc_ref, j - 3)
        return _dot(hb, w_ref[:, c * ATT_DIM:(c + 1) * ATT_DIM])

    lft_ref[...] = _log_sigmoid(_dot_nt(wflt_ref[...], hb) + bf_ref[...])
    for j, heads_ref in enumerate((None, k_ref, v_ref)):
        r = proj(j)
        scaled = r * Q_SCALE if j == 0 else r
        for hp in range(HEAD_PAIRS):
            pairs_ref[j, hp] = scaled[:, hp * LANES:(hp + 1) * LANES].astype(BF16)
        if heads_ref is not None:
            for h in range(ATT_HEADS):
                heads_ref[pl.ds(h, tm, stride=ATT_HEADS), :] = r[:, h * HEAD_DIM:(h + 1) * HEAD_DIM]

    u = proj(4) * proj(3)
    ubuf_ref[PAD_ROWS:PAD_ROWS + tm, :] = u
    u1 = ubuf_ref[PAD_ROWS - 1:PAD_ROWS - 1 + tm, :]
    u2 = ubuf_ref[PAD_ROWS - 2:PAD_ROWS - 2 + tm, :]
    if period is not None:
        pos = lax.broadcasted_iota(jnp.int32, (tm, CONV_DIM), 0) % period
        u1 = jnp.where(pos < 1, f1_ref[...], u1)
        u2 = jnp.where(pos < 2, f2_ref[...], u2)
    conv = u2 * cw_ref[0:1, :] + u1 * cw_ref[1:2, :] + u * cw_ref[2:3, :]
    last = ubuf_ref[tm:tm + PAD_ROWS, :]
    ubuf_ref[0:PAD_ROWS, :] = last
    if period is None:
        tail_ref[0] = last
    else:
        tail_ref[...] = u
    y = proj(5) * conv
    mconv_ref[...] = ((y * _rms_scale(y)) * gout_ref[...]).astype(BF16)


def _in_proj(x, w, tm, seq, fills=None):
    m, d = x.shape
    n = m // tm
    row = lambda i: (i, 0)
    in_specs = [pl.BlockSpec((tm, d), lambda i: (0, 0), pipeline_mode=pl.Buffered(1)),
                pl.BlockSpec((tm, d), lambda i: (jnp.minimum(i + 1, n - 1), 0)),
                _resident((1, d)),
                _resident((d, 3 * ATT_DIM)),
                _resident((d, 3 * CONV_DIM)),
                _resident((ATT_HEADS, d)),
                _resident((ATT_HEADS, 1)),
                _resident((CONV_W, CONV_DIM)),
                _resident((1, CONV_DIM))]
    args = [x, x, w["g_mix"], w["w_qkv"], w["w_conv"], w["w_flt"], w["b_f"], w["conv_w"],
            w["g_conv"]]
    if fills is None:
        blocks_per_seq = seq // tm
        tail_spec = pl.BlockSpec((1, PAD_ROWS, CONV_DIM), lambda i: (i // blocks_per_seq, 0, 0))
        tail_shape = jax.ShapeDtypeStruct((m // seq, PAD_ROWS, CONV_DIM), F32)
        period = None
    else:
        blocks_per_seq = 1
        in_specs += [pl.BlockSpec((tm, CONV_DIM), row), pl.BlockSpec((tm, CONV_DIM), row)]
        args += list(fills)
        tail_spec = pl.BlockSpec((tm, CONV_DIM), row)
        tail_shape = jax.ShapeDtypeStruct((m, CONV_DIM), F32)
        period = seq
    heads_rows = tm * ATT_HEADS * LANES * 4
    vmem = (3 * tm * d * 4 + d * 6 * ATT_DIM * 2 + 2 * 3 * tm * ATT_DIM * 2 + 2 * 2 * heads_rows
            + 2 * tm * d * 2 + 12 * tm * ATT_DIM * 4 + (6 << 20))
    heads_spec = pl.BlockSpec((tm * ATT_HEADS, HEAD_DIM), row)
    heads_shape = jax.ShapeDtypeStruct((m * ATT_HEADS, HEAD_DIM), F32)
    return pl.pallas_call(
        functools.partial(_in_proj_kernel, tm=tm, blocks_per_seq=blocks_per_seq, period=period),
        grid=(n,),
        in_specs=in_specs,
        out_specs=[pl.BlockSpec((3, HEAD_PAIRS, tm, LANES), lambda i: (0, 0, i, 0)),
                   heads_spec, heads_spec,
                   pl.BlockSpec((ATT_HEADS, tm), lambda i: (0, i)),
                   pl.BlockSpec((tm, CONV_DIM), row),
                   tail_spec],
        out_shape=[jax.ShapeDtypeStruct((3, HEAD_PAIRS, m, LANES), BF16),
                   heads_shape, heads_shape,
                   jax.ShapeDtypeStruct((ATT_HEADS, m), F32),
                   jax.ShapeDtypeStruct((m, CONV_DIM), BF16),
                   tail_shape],
        scratch_shapes=[pltpu.VMEM((2, tm, d), BF16),
                        pltpu.VMEM((tm + PAD_ROWS, CONV_DIM), F32)],
        compiler_params=_params(("arbitrary",), vmem),
        name="in_proj",
    )(*args)


def _attn_kernel(q_ref, k_ref, v_ref, lft_ref, g_ref, o_ref,
                 negc_ref, qx_ref, vx_ref, m_ref, acc_ref, *, tq, tk, seq):
    qi = pl.program_id(1)
    n_kv = seq // tk
    diag = tq // tk
    n_chunk = tk // LANES

    @pl.when(qi == 0)
    def _():
        upper = _upper_ones(tk)
        carry = jnp.zeros((ATT_HEADS, 1), F32)
        for kb in range(n_kv):
            c_blk = _prefix_lanes(lft_ref[:, kb * tk:(kb + 1) * tk], upper) + carry
            negc_ref[kb * ATT_HEADS:(kb + 1) * ATT_HEADS, :] = c_blk * -LOG2E
            carry = c_blk[:, tk - 1:tk]

        lane_k = lax.broadcasted_iota(jnp.int32, (tk, LANES), 1)

        def indicator(mask):
            return jnp.where(mask, 1.0, 0.0).astype(BF16)

        keep = (indicator(lane_k < HEAD_DIM), indicator(lane_k >= HEAD_DIM))

        def build(kb, _):
            rows_k = pl.ds(pl.multiple_of(kb * tk, tk), tk)
            for hp in range(HEAD_PAIRS):
                v2 = v_ref[hp, rows_k, :]
                for a in range(2):
                    vx_ref[2 * hp + a, rows_k, :] = v2 * keep[a] + keep[1 - a]
            return 0

        lax.fori_loop(0, n_kv, build, 0)

    is_a = lax.broadcasted_iota(jnp.int32, (tq, LANES), 1) < HEAD_DIM
    for hp in range(HEAD_PAIRS):
        q2 = q_ref[hp]
        zero = jnp.zeros_like(q2)
        qx_ref[hp, 0:tq, :] = jnp.where(is_a, q2, zero)
        qx_ref[hp, tq:2 * tq, :] = jnp.where(is_a, zero, q2)
    m_ref[...] = jnp.full(m_ref.shape, NEG, F32)
    acc_ref[...] = jnp.zeros(acc_ref.shape, F32)

    row = lax.broadcasted_iota(jnp.int32, (tq, LANES), 0)
    col = lax.broadcasted_iota(jnp.int32, (tq, LANES), 1)

    def kv_step(kv, n_blk, masked):
        rows_k = pl.ds(pl.multiple_of(kv * tk, tk), n_blk * tk)
        col_shift = kv * tk - qi * tq
        for hp in range(HEAD_PAIRS):
            s2 = _dot_nt(qx_ref[hp], k_ref[hp, rows_k, :])
            for a in range(2):
                h = 2 * hp + a
                chunks = []
                for blk in range(n_blk):
                    bias = negc_ref[pl.ds((kv + blk) * ATT_HEADS + h, 1), :]
                    for c in range(n_chunk):
                        off = blk * tk + c * LANES
                        sc = (s2[a * tq:(a + 1) * tq, off:off + LANES]
                              + bias[:, c * LANES:(c + 1) * LANES])
                        if masked:
                            sc = jnp.where(col + (col_shift + off) <= row, sc, NEG)
                        chunks.append(sc)
                mx = chunks[0]
                for sc in chunks[1:]:
                    mx = jnp.maximum(mx, sc)
                m_old = m_ref[h]
                m_new = jnp.maximum(m_old, jnp.max(mx, axis=-1, keepdims=True))
                alpha = jnp.exp2(m_old - m_new)
                p = jnp.concatenate([jnp.exp2(sc - m_new) for sc in chunks], axis=1).astype(BF16)
                acc_ref[h] = alpha * acc_ref[h] + _dot(p, vx_ref[h, rows_k, :])
                m_ref[h] = m_new

    def pair_step(j, _):
        kv_step(2 * j, 2, False)
        return 0

    def full_step(kv, _):
        kv_step(kv, 1, False)
        return 0

    def diag_step(kv, _):
        kv_step(kv, 1, True)
        return 0

    n_full = qi * diag
    n_pair = lax.shift_right_logical(n_full, 1)
    lax.fori_loop(0, n_pair, pair_step, 0)
    lax.fori_loop(2 * n_pair, n_full, full_step, 0)
    lax.fori_loop(n_full, n_full + diag, diag_step, 0)

    outs = []
    ss = jnp.zeros((tq, LANES), F32)
    for hp in range(HEAD_PAIRS):
        acc_a, acc_b = acc_ref[2 * hp], acc_ref[2 * hp + 1]
        den = pltpu.roll(jnp.where(is_a, acc_b, acc_a), HEAD_DIM, axis=1)
        o = jnp.where(is_a, acc_a, acc_b) / den
        outs.append(o)
        ss = ss + o * o
    inv = lax.rsqrt(jnp.sum(ss, axis=-1, keepdims=True) * (1.0 / ATT_DIM) + EPS)
    for hp in range(HEAD_PAIRS):
        sl = slice(hp * LANES, (hp + 1) * LANES)
        o_ref[:, sl] = ((outs[hp] * inv) * g_ref[:, sl]).astype(BF16)


def _attn_prompt(qkv_pairs, lft, g_att, batch, seq, tq, tk):
    m = batch * seq
    nq = seq // tq
    vmem = (2 * 2 * seq * ATT_DIM * 2 + seq * ATT_DIM * 2 * 2 + 2 * ATT_HEADS * tq * LANES * 4
            + 4 * tq * ATT_DIM * 2 + (12 << 20))
    return pl.pallas_call(
        functools.partial(_attn_kernel, tq=tq, tk=tk, seq=seq),
        grid=(batch, nq),
        in_specs=[pl.BlockSpec((None, HEAD_PAIRS, tq, LANES), lambda b, i: (0, 0, b * nq + i, 0)),
                  pl.BlockSpec((None, HEAD_PAIRS, seq, LANES), lambda b, i: (1, 0, b, 0)),
                  pl.BlockSpec((None, HEAD_PAIRS, seq, LANES), lambda b, i: (2, 0, b, 0)),
                  pl.BlockSpec((ATT_HEADS, seq), lambda b, i: (0, b)),
                  pl.BlockSpec((1, ATT_DIM), lambda b, i: (0, 0))],
        out_specs=pl.BlockSpec((tq, ATT_DIM), lambda b, i: (b * nq + i, 0)),
        out_shape=jax.ShapeDtypeStruct((m, ATT_DIM), BF16),
        scratch_shapes=[pltpu.VMEM((seq // tk * ATT_HEADS, tk), F32),
                        pltpu.VMEM((HEAD_PAIRS, 2 * tq, LANES), BF16),
                        pltpu.VMEM((ATT_HEADS, seq, LANES), BF16),
                        pltpu.VMEM((ATT_HEADS, tq, LANES), F32),
                        pltpu.VMEM((ATT_HEADS, tq, LANES), F32)],
        compiler_params=_params(("arbitrary", "arbitrary"), vmem),
        name="attn_prompt",
    )(qkv_pairs, qkv_pairs, qkv_pairs, lft, g_att)


def _sattn_kernel(pt_ref, q_ref, kn_ref, vn_ref, lfn_ref, g_ref, *refs, pps, dec):
    del pt_ref
    k_refs, v_refs, lf_refs = refs[:pps], refs[pps:2 * pps], refs[2 * pps:3 * pps]
    o_ref, qbd_ref, m_ref, l_ref, acc_ref, carry_ref = refs[3 * pps:]
    step = pl.program_id(1)
    rows_q = dec * ATT_HEADS
    page = k_refs[0].shape[2]

    @pl.when(step == 0)
    def _():
        q = q_ref[0].astype(F32)
        head_of_col = lax.broadcasted_iota(jnp.int32, (ATT_HEADS, ATT_DIM), 1) // HEAD_DIM
        head_of_row = lax.broadcasted_iota(jnp.int32, (ATT_HEADS, ATT_DIM), 0)
        own = head_of_col == head_of_row
        for t in range(dec):
            qt = jnp.broadcast_to(q[t:t + 1, :], (ATT_HEADS, ATT_DIM))
            qbd_ref[t * ATT_HEADS:(t + 1) * ATT_HEADS, :] = jnp.where(own, qt, 0.0)
        m_ref[...] = jnp.full(m_ref.shape, NEG, F32)
        l_ref[...] = jnp.zeros(l_ref.shape, F32)
        acc_ref[...] = jnp.zeros(acc_ref.shape, F32)
        carry_ref[...] = jnp.zeros(carry_ref.shape, F32)

    upper = _upper_ones(page)

    def update(scores, values):
        m_old = m_ref[...]
        m_new = m_old
        for s in scores:
            m_new = jnp.maximum(m_new, jnp.max(s, axis=-1, keepdims=True))
        alpha = jnp.exp2(m_old - m_new)
        l_new = alpha * l_ref[...]
        pv = None
        for s, v in zip(scores, values):
            p = jnp.exp2(s - m_new)
            l_new = l_new + jnp.sum(p, axis=-1, keepdims=True)
            d = _dot_nt(p, v)
            pv = d if pv is None else pv + d
        acc_ref[...] = alpha * acc_ref[...] + pv
        l_ref[...] = l_new
        m_ref[...] = m_new

    def scores_of(k_page, c_page):
        return _dot(qbd_ref[...], k_page) - jnp.tile(c_page * LOG2E, (dec, 1))

    lf_all = jnp.concatenate([lf_refs[r][0] for r in range(pps)], axis=0)
    within = _prefix_lanes(lf_all, upper)
    car = carry_ref[...]
    scores = []
    for r in range(pps):
        blk = within[r * ATT_HEADS:(r + 1) * ATT_HEADS, :]
        scores.append(scores_of(k_refs[r][0], blk + car))
        car = car + blk[:, page - 1:page]
    carry_ref[...] = car
    update(scores, [v_refs[r][0] for r in range(pps)])

    @pl.when(step == pl.num_programs(1) - 1)
    def _():
        c_new = _prefix_lanes(lfn_ref[0], upper) + carry_ref[...]
        s = scores_of(kn_ref[0], c_new)
        t_of_row = lax.broadcasted_iota(jnp.int32, (rows_q, page), 0) // ATT_HEADS
        key = lax.broadcasted_iota(jnp.int32, (rows_q, page), 1)
        update([jnp.where(key <= t_of_row, s, NEG)], [vn_ref[0]])
        o = acc_ref[...] / l_ref[...]
        head_of_col = lax.broadcasted_iota(jnp.int32, (rows_q, ATT_DIM), 1) // HEAD_DIM
        head_of_row = lax.broadcasted_iota(jnp.int32, (rows_q, ATT_DIM), 0) % ATT_HEADS
        o = jnp.where(head_of_col == head_of_row, o, 0.0)
        y = jnp.sum(o.reshape(dec, ATT_HEADS, ATT_DIM), axis=1)
        o_ref[0] = ((y * _rms_scale(y)) * g_ref[...]).astype(BF16)


SAMPLE_PAGES_PER_STEP = 16


def _attn_sample(page_table, q_s, k_new, v_new, lf_new, g_att, cache_k, cache_v, cache_lft, pps):
    nb, n_pages = page_table.shape
    pps = pps if n_pages % pps == 0 else n_pages
    dec = q_s.shape[1]
    page = cache_k.shape[2]
    rows_q = dec * ATT_HEADS
    steps = n_pages // pps

    def paged(r):
        return lambda b, s, pt: (pt[b, s * pps + r], 0, 0)

    per_seq = lambda b, s, pt: (b, 0, 0)
    in_specs = [pl.BlockSpec((1, dec, ATT_DIM), per_seq),
                pl.BlockSpec((1, ATT_DIM, page), per_seq),
                pl.BlockSpec((1, ATT_DIM, page), per_seq),
                pl.BlockSpec((1, ATT_HEADS, page), per_seq),
                pl.BlockSpec((1, ATT_DIM), lambda b, s, pt: (0, 0))]
    in_specs += [pl.BlockSpec((1, ATT_DIM, page), paged(r)) for r in range(pps)]
    in_specs += [pl.BlockSpec((1, ATT_DIM, page), paged(r)) for r in range(pps)]
    in_specs += [pl.BlockSpec((1, ATT_HEADS, page), paged(r)) for r in range(pps)]
    vmem = 2 * 2 * pps * page * ATT_DIM * 4 + 4 * page * ATT_DIM * 4 + 16 * rows_q * ATT_DIM * 4 + (8 << 20)
    grid_spec = pltpu.PrefetchScalarGridSpec(
        num_scalar_prefetch=1,
        grid=(nb, steps),
        in_specs=in_specs,
        out_specs=pl.BlockSpec((1, dec, ATT_DIM), per_seq),
        scratch_shapes=[pltpu.VMEM((rows_q, ATT_DIM), F32),
                        pltpu.VMEM((rows_q, 1), F32),
                        pltpu.VMEM((rows_q, 1), F32),
                        pltpu.VMEM((rows_q, ATT_DIM), F32),
                        pltpu.VMEM((ATT_HEADS, 1), F32)])
    return pl.pallas_call(
        functools.partial(_sattn_kernel, pps=pps, dec=dec),
        grid_spec=grid_spec,
        out_shape=jax.ShapeDtypeStruct((nb, dec, ATT_DIM), BF16),
        compiler_params=_params(("arbitrary", "arbitrary"), vmem),
        name="attn_sample",
    )(page_table, q_s, k_new, v_new, lf_new, g_att,
      *([cache_k] * pps), *([cache_v] * pps), *([cache_lft] * pps))


def _out_kernel(x_ref, ma_ref, mc_ref, wa_ref, wc_ref, g_ref, x1_ref, h2_ref):
    x1 = x_ref[...] + (_dot(ma_ref[...], wa_ref[...]) + _dot(mc_ref[...], wc_ref[...]))
    x1_ref[...] = x1
    h2_ref[...] = ((x1 * _rms_scale(x1)) * g_ref[...]).astype(BF16)


def _out_proj(x, m_att, m_conv, w_a, w_c, g_mlp, tm):
    m, d = x.shape
    row = lambda i: (i, 0)
    const = lambda i: (0, 0)
    vmem = 2 * 2 * tm * d * 4 + 2 * tm * d * 2 + 4 * tm * ATT_DIM * 2 + 2 * 2 * ATT_DIM * d * 2 + (8 << 20)
    return pl.pallas_call(
        _out_kernel,
        grid=(m // tm,),
        in_specs=[pl.BlockSpec((tm, d), row),
                  pl.BlockSpec((tm, ATT_DIM), row),
                  pl.BlockSpec((tm, CONV_DIM), row),
                  pl.BlockSpec((ATT_DIM, d), const),
                  pl.BlockSpec((CONV_DIM, d), const),
                  pl.BlockSpec((1, d), const)],
        out_specs=[pl.BlockSpec((tm, d), row), pl.BlockSpec((tm, d), row)],
        out_shape=[jax.ShapeDtypeStruct((m, d), F32), jax.ShapeDtypeStruct((m, d), BF16)],
        compiler_params=_params(("arbitrary",), vmem),
        name="out_proj",
    )(x, m_att, m_conv, w_a, w_c, g_mlp)


def _mlp_kernel(h_ref, x1_ref, wu_ref, wd_ref, g_ref, y_ref, acc_ref):
    f = pl.program_id(1)

    @pl.when(f == 0)
    def _():
        acc_ref[...] = x1_ref[...]

    z = jnp.maximum(_dot(h_ref[...], wu_ref[...]), 0.0)
    acc_ref[...] += _dot((z * z).astype(BF16), wd_ref[...])

    @pl.when(f == pl.num_programs(1) - 1)
    def _():
        x2 = acc_ref[...]
        y_ref[...] = (x2 * _rms_scale(x2)) * g_ref[...]


def _mlp(h2, x1, w_up, w_down, g_final, tm, tf):
    m, d = x1.shape
    ff = w_up.shape[1]
    row = lambda i, f: (i, 0)
    vmem = (2 * tm * d * 2 + 2 * tm * d * 4 + 2 * 2 * d * tf * 2 + 2 * tm * d * 4 + tm * d * 4
            + 3 * tm * tf * 4 + (6 << 20))
    return pl.pallas_call(
        _mlp_kernel,
        grid=(m // tm, ff // tf),
        in_specs=[pl.BlockSpec((tm, d), row),
                  pl.BlockSpec((tm, d), row),
                  pl.BlockSpec((d, tf), lambda i, f: (0, f)),
                  pl.BlockSpec((tf, d), lambda i, f: (f, 0)),
                  pl.BlockSpec((1, d), lambda i, f: (0, 0))],
        out_specs=pl.BlockSpec((tm, d), row),
        out_shape=jax.ShapeDtypeStruct((m, d), F32),
        scratch_shapes=[pltpu.VMEM((tm, d), F32)],
        compiler_params=_params(("arbitrary", "arbitrary"), vmem),
        name="mlp",
    )(h2, x1, w_up, w_down, g_final)


def _keys_last(cache):
    n_pool, page = cache.shape[:2]
    return cache.transpose(0, 2, 3, 1).reshape(n_pool, ATT_DIM, page)


def _row_tile(m, want):
    return want if m % want == 0 else m


def _layer_weights(norm_mix, w_in, b_f, conv_w, norm_att_out, norm_conv_out, w_out, norm_mlp,
                   w_up, w_down):
    a = ATT_DIM
    c0 = 3 * a + ATT_HEADS
    w_qkv = w_in[:, :3 * a].astype(BF16)
    w_flt = w_in[:, 3 * a:c0].T.astype(BF16)
    w_conv = jnp.concatenate([w_in[:, c0:c0 + CONV_DIM],
                              w_in[:, c0 + 2 * CONV_DIM:c0 + 3 * CONV_DIM],
                              w_in[:, c0 + CONV_DIM:c0 + 2 * CONV_DIM]],
                             axis=1).astype(BF16)
    return dict(g_mix=norm_mix[None, :], w_qkv=w_qkv, w_conv=w_conv, w_flt=w_flt, b_f=b_f[:, None], conv_w=conv_w, g_att=norm_att_out[None, :],
                g_conv=norm_conv_out[None, :], w_a=w_out[:a].astype(BF16),
                w_c=w_out[a:].astype(BF16), g_mlp=norm_mlp[None, :],
                w_up=w_up.astype(BF16), w_down=w_down.astype(BF16))


def _tail(x, m_att, m_conv, w, g_final, tm, tf):
    x1, h2 = _out_proj(x, m_att, m_conv, w["w_a"], w["w_c"], w["g_mlp"], tm)
    return _mlp(h2, x1, w["w_up"], w["w_down"], g_final, tm, _row_tile(w["w_up"].shape[1], tf))


def kernel(x_prompt, x_sample, cache_k, cache_v, cache_logf, state_conv, page_table, norm_mix, w_in,
           b_f, conv_w, norm_att_out, norm_conv_out, w_out, norm_mlp, w_up, w_down, norm_final):
    depth = w_in.shape[0]
    assert depth == 1, "single-layer step"
    batch, seq, d = x_prompt.shape
    nb, dec, _ = x_sample.shape
    page = cache_k.shape[2]
    w = _layer_weights(norm_mix[0], w_in[0], b_f[0], conv_w[0], norm_att_out[0], norm_conv_out[0],
                       w_out[0], norm_mlp[0], w_up[0], w_down[0])
    g_final = norm_final[None, :]

    xp = x_prompt.reshape(batch * seq, d)
    pairs_p, k_p, v_p, lft_p, m_conv, tail_p = _in_proj(xp, w, _row_tile(seq, 256), seq)
    tq = _row_tile(seq, 256)
    m_att = _attn_prompt(pairs_p, lft_p, w["g_att"], batch, seq, tq, tq)
    y_p = _tail(xp, m_att, m_conv, w, g_final, _row_tile(seq, 512), 1024)

    ms = nb * dec
    xs = x_sample.reshape(ms, d)
    st = state_conv[0]
    zeros = jnp.zeros((nb, dec - 1, CONV_DIM), F32)
    f1 = jnp.concatenate([st[:, 1:2], zeros], axis=1).reshape(ms, CONV_DIM)
    f2 = jnp.concatenate([st, zeros[:, 1:]], axis=1).reshape(ms, CONV_DIM)
    pairs_s, k_s, v_s, lft_s, m_conv_s, u_s = _in_proj(xs, w, ms, dec, fills=(f1, f2))
    q_s = pairs_s[0].transpose(1, 0, 2).reshape(nb, dec, ATT_DIM)
    pad = ((0, 0), (0, 0), (0, page - dec))
    k_new = jnp.pad(k_s.reshape(nb, dec, ATT_DIM).transpose(0, 2, 1), pad)
    v_new = jnp.pad(v_s.reshape(nb, dec, ATT_DIM).transpose(0, 2, 1), pad)
    lf_new = jnp.pad(lft_s.reshape(ATT_HEADS, nb, dec).transpose(1, 0, 2), pad)
    n_pool = cache_k.shape[1]
    m_att_s = _attn_sample(page_table, q_s, k_new, v_new, lf_new, w["g_att"],
                           _keys_last(cache_k[0]), _keys_last(cache_v[0]),
                           cache_logf[0].transpose(0, 2, 1), SAMPLE_PAGES_PER_STEP)
    y_s = _tail(xs, m_att_s.reshape(ms, ATT_DIM), m_conv_s, w, g_final, ms, 1024)

    hd = (ATT_HEADS, HEAD_DIM)
    return (y_p.reshape(batch, seq, d),
            y_s.reshape(nb, dec, d),
            k_p.reshape(1, batch, seq, *hd),
            v_p.reshape(1, batch, seq, *hd),
            lft_p.T.reshape(1, batch, seq, ATT_HEADS),
            tail_p[:, PAD_ROWS - (CONV_W - 1):][None],
            k_s.reshape(1, nb, dec, *hd),
            v_s.reshape(1, nb, dec, *hd),
            lft_s.T.reshape(1, nb, dec, ATT_HEADS),
            u_s.reshape(nb, dec, CONV_DIM)[:, dec - (CONV_W - 1):][None])
```

```python
import functools

import jax
import jax.numpy as jnp
from jax import lax
from jax.experimental import pallas as pl
from jax.experimental.pallas import tpu as pltpu

F32 = jnp.float32
BF16 = jnp.bfloat16

ATT_HEADS = 16
HEAD_DIM = 64
ATT_DIM = ATT_HEADS * HEAD_DIM
CONV_DIM = 1024
CONV_W = 3
EPS = 1e-6
LANES = 128
HEAD_PAIRS = ATT_DIM // LANES
LOG2E = 1.4426950408889634
Q_SCALE = HEAD_DIM ** -0.5 * LOG2E
NEG = -1e30
V7X_VMEM_LIMIT = 56 * 1024 * 1024


def _params(semantics, vmem_bytes):
    return pltpu.CompilerParams(dimension_semantics=semantics,
                                vmem_limit_bytes=min(int(vmem_bytes), V7X_VMEM_LIMIT))


def _dot(a, b):
    return jnp.dot(a, b, preferred_element_type=F32)


def _dot_nt(a, b):
    return lax.dot_general(a, b, (((1,), (1,)), ((), ())), preferred_element_type=F32)


def _split_bf16(x):
    hi = x.astype(BF16)
    lo = (x - hi.astype(F32)).astype(BF16)
    return hi, lo


def _upper_ones(n):
    r = lax.broadcasted_iota(jnp.int32, (n, n), 0)
    c = lax.broadcasted_iota(jnp.int32, (n, n), 1)
    return (r <= c).astype(BF16)


def _prefix_lanes(x, upper):
    hi, lo = _split_bf16(x)
    return _dot(hi, upper) + _dot(lo, upper)


def _rms_scale(x):
    return lax.rsqrt(jnp.mean(x * x, axis=-1, keepdims=True) + EPS)


def _log_sigmoid(z):
    return jnp.minimum(z, 0.0) - jnp.log1p(jnp.exp(-jnp.abs(z)))


PAD_ROWS = 8


def _resident(shape):
    return pl.BlockSpec(shape, lambda i: (0,) * len(shape), pipeline_mode=pl.Buffered(1))


def _in_proj_kernel(*refs, tm, blocks_per_seq, period):
    if period is None:
        (x0_ref, xn_ref, g_ref, wa_ref, wc_ref, wflt_ref, bf_ref, cw_ref, gout_ref,
         pairs_ref, k_ref, v_ref, lft_ref, mconv_ref, tail_ref, h_ref, ubuf_ref) = refs
    else:
        (x0_ref, xn_ref, g_ref, wa_ref, wc_ref, wflt_ref, bf_ref, cw_ref, gout_ref, f1_ref, f2_ref,
         pairs_ref, k_ref, v_ref, lft_ref, mconv_ref, tail_ref, h_ref, ubuf_ref) = refs
    i = pl.program_id(0)

    def normed(x):
        return ((x * _rms_scale(x)) * g_ref[...]).astype(BF16)

    @pl.when(i == 0)
    def _():
        h_ref[0] = normed(x0_ref[...])

    @pl.when(i % blocks_per_seq == 0)
    def _():
        ubuf_ref[0:PAD_ROWS, :] = jnp.zeros((PAD_ROWS, CONV_DIM), F32)

    slot = i % 2
    hb = h_ref[slot]
    h_ref[1 - slot] = normed(xn_ref[...])

    def proj(j):
        w_ref, c = (wa_ref, j) if j < 3 else (wc_ref, j - 3)
        return _dot(hb, w_ref[:, c * ATT_DIM:(c + 1) * ATT_DIM])

    lft_ref[...] = _log_sigmoid(_dot_nt(wflt_ref[...], hb) + bf_ref[...])
    for j, heads_ref in enumerate((None, k_ref, v_ref)):
        r = proj(j)
        scaled = r * Q_SCALE if j == 0 else r
        for hp in range(HEAD_PAIRS):
            pairs_ref[j, hp] = scaled[:, hp * LANES:(hp + 1) * LANES].astype(BF16)
        if heads_ref is not None:
            for h in range(ATT_HEADS):
                heads_ref[pl.ds(h, tm, stride=ATT_HEADS), :] = r[:, h * HEAD_DIM:(h + 1) * HEAD_DIM]

    u = proj(4) * proj(3)
    ubuf_ref[PAD_ROWS:PAD_ROWS + tm, :] = u
    u1 = ubuf_ref[PAD_ROWS - 1:PAD_ROWS - 1 + tm, :]
    u2 = ubuf_ref[PAD_ROWS - 2:PAD_ROWS - 2 + tm, :]
    if period is not None:
        pos = lax.broadcasted_iota(jnp.int32, (tm, CONV_DIM), 0) % period
        u1 = jnp.where(pos < 1, f1_ref[...], u1)
        u2 = jnp.where(pos < 2, f2_ref[...], u2)
    conv = u2 * cw_ref[0:1, :] + u1 * cw_ref[1:2, :] + u * cw_ref[2:3, :]
    last = ubuf_ref[tm:tm + PAD_ROWS, :]
    ubuf_ref[0:PAD_ROWS, :] = last
    if period is None:
        tail_ref[0] = last
    else:
        tail_ref[...] = u
    y = proj(5) * conv
    mconv_ref[...] = ((y * _rms_scale(y)) * gout_ref[...]).astype(BF16)


def _in_proj(x, w, tm, seq, fills=None):
    m, d = x.shape
    n = m // tm
    row = lambda i: (i, 0)
    in_specs = [pl.BlockSpec((tm, d), lambda i: (0, 0), pipeline_mode=pl.Buffered(1)),
                pl.BlockSpec((tm, d), lambda i: (jnp.minimum(i + 1, n - 1), 0)),
                _resident((1, d)),
                _resident((d, 3 * ATT_DIM)),
                _resident((d, 3 * CONV_DIM)),
                _resident((ATT_HEADS, d)),
                _resident((ATT_HEADS, 1)),
                _resident((CONV_W, CONV_DIM)),
                _resident((1, CONV_DIM))]
    args = [x, x, w["g_mix"], w["w_qkv"], w["w_conv"], w["w_flt"], w["b_f"], w["conv_w"],
            w["g_conv"]]
    if fills is None:
        blocks_per_seq = seq // tm
        tail_spec = pl.BlockSpec((1, PAD_ROWS, CONV_DIM), lambda i: (i // blocks_per_seq, 0, 0))
        tail_shape = jax.ShapeDtypeStruct((m // seq, PAD_ROWS, CONV_DIM), F32)
        period = None
    else:
        blocks_per_seq = 1
        in_specs += [pl.BlockSpec((tm, CONV_DIM), row), pl.BlockSpec((tm, CONV_DIM), row)]
        args += list(fills)
        tail_spec = pl.BlockSpec((tm, CONV_DIM), row)
        tail_shape = jax.ShapeDtypeStruct((m, CONV_DIM), F32)
        period = seq
    heads_rows = tm * ATT_HEADS * LANES * 4
    vmem = (3 * tm * d * 4 + d * 6 * ATT_DIM * 2 + 2 * 3 * tm * ATT_DIM * 2 + 2 * 2 * heads_rows
            + 2 * tm * d * 2 + 12 * tm * ATT_DIM * 4 + (6 << 20))
    heads_spec = pl.BlockSpec((tm * ATT_HEADS, HEAD_DIM), row)
    heads_shape = jax.ShapeDtypeStruct((m * ATT_HEADS, HEAD_DIM), F32)
    return pl.pallas_call(
        functools.partial(_in_proj_kernel, tm=tm, blocks_per_seq=blocks_per_seq, period=period),
        grid=(n,),
        in_specs=in_specs,
        out_specs=[pl.BlockSpec((3, HEAD_PAIRS, tm, LANES), lambda i: (0, 0, i, 0)),
                   heads_spec, heads_spec,
                   pl.BlockSpec((ATT_HEADS, tm), lambda i: (0, i)),
                   pl.BlockSpec((tm, CONV_DIM), row),
                   tail_spec],
        out_shape=[jax.ShapeDtypeStruct((3, HEAD_PAIRS, m, LANES), BF16),
                   heads_shape, heads_shape,
                   jax.ShapeDtypeStruct((ATT_HEADS, m), F32),
                   jax.ShapeDtypeStruct((m, CONV_DIM), BF16),
                   tail_shape],
        scratch_shapes=[pltpu.VMEM((2, tm, d), BF16),
                        pltpu.VMEM((tm + PAD_ROWS, CONV_DIM), F32)],
        compiler_params=_params(("arbitrary",), vmem),
        name="in_proj",
    )(*args)


def _attn_kernel(q_ref, k_ref, v_ref, lft_ref, g_ref, o_ref,
                 negc_ref, qx_ref, vx_ref, m_ref, acc_ref, *, tq, tk, seq):
    qi = pl.program_id(1)
    n_kv = seq // tk
    diag = tq // tk
    n_chunk = tk // LANES

    @pl.when(qi == 0)
    def _():
        upper = _upper_ones(tk)
        carry = jnp.zeros((ATT_HEADS, 1), F32)
        for kb in range(n_kv):
            c_blk = _prefix_lanes(lft_ref[:, kb * tk:(kb + 1) * tk], upper) + carry
            negc_ref[kb * ATT_HEADS:(kb + 1) * ATT_HEADS, :] = c_blk * -LOG2E
            carry = c_blk[:, tk - 1:tk]

        lane_k = lax.broadcasted_iota(jnp.int32, (tk, LANES), 1)

        def indicator(mask):
            return jnp.where(mask, 1.0, 0.0).astype(BF16)

        keep = (indicator(lane_k < HEAD_DIM), indicator(lane_k >= HEAD_DIM))

        def build(kb, _):
            rows_k = pl.ds(pl.multiple_of(kb * tk, tk), tk)
            for hp in range(HEAD_PAIRS):
                v2 = v_ref[hp, rows_k, :]
                for a in range(2):
                    vx_ref[2 * hp + a, rows_k, :] = v2 * keep[a] + keep[1 - a]
            return 0

        lax.fori_loop(0, n_kv, build, 0)

    is_a = lax.broadcasted_iota(jnp.int32, (tq, LANES), 1) < HEAD_DIM
    for hp in range(HEAD_PAIRS):
        q2 = q_ref[hp]
        zero = jnp.zeros_like(q2)
        qx_ref[hp, 0:tq, :] = jnp.where(is_a, q2, zero)
        qx_ref[hp, tq:2 * tq, :] = jnp.where(is_a, zero, q2)
    m_ref[...] = jnp.full(m_ref.shape, NEG, F32)
    acc_ref[...] = jnp.zeros(acc_ref.shape, F32)

    def kv_step(kv, n_blk, masked, row_lo=0):
        rows_k = pl.ds(pl.multiple_of(kv * tk, tk), n_blk * tk)
        col_shift = kv * tk - qi * tq
        n_row = tq - row_lo
        rows_q = slice(row_lo, tq)
        if masked:
            row = lax.broadcasted_iota(jnp.int32, (n_row, LANES), 0) + row_lo
            col = lax.broadcasted_iota(jnp.int32, (n_row, LANES), 1)
        for hp in range(HEAD_PAIRS):
            if row_lo:
                lhs = jnp.concatenate([qx_ref[hp, rows_q, :], qx_ref[hp, tq + row_lo:2 * tq, :]], axis=0)
            else:
                lhs = qx_ref[hp]
            s2 = _dot_nt(lhs, k_ref[hp, rows_k, :])
            for a in range(2):
                h = 2 * hp + a
                chunks = []
                for blk in range(n_blk):
                    bias = negc_ref[pl.ds((kv + blk) * ATT_HEADS + h, 1), :]
                    for c in range(n_chunk):
                        off = blk * tk + c * LANES
                        sc = (s2[a * n_row:(a + 1) * n_row, off:off + LANES]
                              + bias[:, c * LANES:(c + 1) * LANES])
                        if masked:
                            sc = jnp.where(col + (col_shift + off) <= row, sc, NEG)
                        chunks.append(sc)
                mx = chunks[0]
                for sc in chunks[1:]:
                    mx = jnp.maximum(mx, sc)
                m_old = m_ref[h, rows_q, :]
                m_new = jnp.maximum(m_old, jnp.max(mx, axis=-1, keepdims=True))
                alpha = jnp.exp2(m_old - m_new)
                p = jnp.concatenate([jnp.exp2(sc - m_new) for sc in chunks], axis=1).astype(BF16)
                acc_ref[h, rows_q, :] = alpha * acc_ref[h, rows_q, :] + _dot(p, vx_ref[h, rows_k, :])
                m_ref[h, rows_q, :] = m_new

    def pair_step(j, _):
        kv_step(2 * j, 2, False)
        return 0

    def full_step(kv, _):
        kv_step(kv, 1, False)
        return 0

    n_full = qi * diag
    n_pair = lax.shift_right_logical(n_full, 1)
    lax.fori_loop(0, n_pair, pair_step, 0)
    lax.fori_loop(2 * n_pair, n_full, full_step, 0)
    for d in range(diag):
        def diag_step(kv, _, row_lo=d * tk):
            kv_step(kv, 1, True, row_lo)
            return 0

        lax.fori_loop(n_full + d, n_full + d + 1, diag_step, 0)

    outs = []
    ss = jnp.zeros((tq, LANES), F32)
    for hp in range(HEAD_PAIRS):
        acc_a, acc_b = acc_ref[2 * hp], acc_ref[2 * hp + 1]
        den = pltpu.roll(jnp.where(is_a, acc_b, acc_a), HEAD_DIM, axis=1)
        o = jnp.where(is_a, acc_a, acc_b) / den
        outs.append(o)
        ss = ss + o * o
    inv = lax.rsqrt(jnp.sum(ss, axis=-1, keepdims=True) * (1.0 / ATT_DIM) + EPS)
    for hp in range(HEAD_PAIRS):
        sl = slice(hp * LANES, (hp + 1) * LANES)
        o_ref[:, sl] = ((outs[hp] * inv) * g_ref[:, sl]).astype(BF16)


def _attn_prompt(qkv_pairs, lft, g_att, batch, seq, tq, tk):
    m = batch * seq
    nq = seq // tq
    vmem = (2 * 2 * seq * ATT_DIM * 2 + seq * ATT_DIM * 2 * 2 + 2 * ATT_HEADS * tq * LANES * 4
            + 4 * tq * ATT_DIM * 2 + (12 << 20))
    return pl.pallas_call(
        functools.partial(_attn_kernel, tq=tq, tk=tk, seq=seq),
        grid=(batch, nq),
        in_specs=[pl.BlockSpec((None, HEAD_PAIRS, tq, LANES), lambda b, i: (0, 0, b * nq + i, 0)),
                  pl.BlockSpec((None, HEAD_PAIRS, seq, LANES), lambda b, i: (1, 0, b, 0)),
                  pl.BlockSpec((None, HEAD_PAIRS, seq, LANES), lambda b, i: (2, 0, b, 0)),
                  pl.BlockSpec((ATT_HEADS, seq), lambda b, i: (0, b)),
                  pl.BlockSpec((1, ATT_DIM), lambda b, i: (0, 0))],
        out_specs=pl.BlockSpec((tq, ATT_DIM), lambda b, i: (b * nq + i, 0)),
        out_shape=jax.ShapeDtypeStruct((m, ATT_DIM), BF16),
        scratch_shapes=[pltpu.VMEM((seq // tk * ATT_HEADS, tk), F32),
                        pltpu.VMEM((HEAD_PAIRS, 2 * tq, LANES), BF16),
                        pltpu.VMEM((ATT_HEADS, seq, LANES), BF16),
                        pltpu.VMEM((ATT_HEADS, tq, LANES), F32),
                        pltpu.VMEM((ATT_HEADS, tq, LANES), F32)],
        compiler_params=_params(("arbitrary", "arbitrary"), vmem),
        name="attn_prompt",
    )(qkv_pairs, qkv_pairs, qkv_pairs, lft, g_att)


def _sattn_kernel(pt_ref, q_ref, kn_ref, vn_ref, lfn_ref, g_ref, *refs, pps, dec):
    del pt_ref
    k_refs, v_refs, lf_refs = refs[:pps], refs[pps:2 * pps], refs[2 * pps:3 * pps]
    o_ref, qbd_ref, m_ref, l_ref, acc_ref, carry_ref = refs[3 * pps:]
    step = pl.program_id(1)
    rows_q = dec * ATT_HEADS
    page = k_refs[0].shape[2]

    @pl.when(step == 0)
    def _():
        q = q_ref[0].astype(F32)
        head_of_col = lax.broadcasted_iota(jnp.int32, (ATT_HEADS, ATT_DIM), 1) // HEAD_DIM
        head_of_row = lax.broadcasted_iota(jnp.int32, (ATT_HEADS, ATT_DIM), 0)
        own = head_of_col == head_of_row
        for t in range(dec):
            qt = jnp.broadcast_to(q[t:t + 1, :], (ATT_HEADS, ATT_DIM))
            qbd_ref[t * ATT_HEADS:(t + 1) * ATT_HEADS, :] = jnp.where(own, qt, 0.0)
        m_ref[...] = jnp.full(m_ref.shape, NEG, F32)
        l_ref[...] = jnp.zeros(l_ref.shape, F32)
        acc_ref[...] = jnp.zeros(acc_ref.shape, F32)
        carry_ref[...] = jnp.zeros(carry_ref.shape, F32)

    upper = _upper_ones(page)

    def update(scores, values):
        m_old = m_ref[...]
        m_new = m_old
        for s in scores:
            m_new = jnp.maximum(m_new, jnp.max(s, axis=-1, keepdims=True))
        alpha = jnp.exp2(m_old - m_new)
        l_new = alpha * l_ref[...]
        pv = None
        for s, v in zip(scores, values):
            p = jnp.exp2(s - m_new)
            l_new = l_new + jnp.sum(p, axis=-1, keepdims=True)
            d = _dot_nt(p, v)
            pv = d if pv is None else pv + d
        acc_ref[...] = alpha * acc_ref[...] + pv
        l_ref[...] = l_new
        m_ref[...] = m_new

    def scores_of(k_page, c_page):
        return _dot(qbd_ref[...], k_page) - jnp.tile(c_page * LOG2E, (dec, 1))

    lf_all = jnp.concatenate([lf_refs[r][0] for r in range(pps)], axis=0)
    within = _prefix_lanes(lf_all, upper)
    car = carry_ref[...]
    scores = []
    for r in range(pps):
        blk = within[r * ATT_HEADS:(r + 1) * ATT_HEADS, :]
        scores.append(scores_of(k_refs[r][0], blk + car))
        car = car + blk[:, page - 1:page]
    carry_ref[...] = car
    update(scores, [v_refs[r][0] for r in range(pps)])

    @pl.when(step == pl.num_programs(1) - 1)
    def _():
        c_new = _prefix_lanes(lfn_ref[0], upper) + carry_ref[...]
        s = scores_of(kn_ref[0], c_new)
        t_of_row = lax.broadcasted_iota(jnp.int32, (rows_q, page), 0) // ATT_HEADS
        key = lax.broadcasted_iota(jnp.int32, (rows_q, page), 1)
        update([jnp.where(key <= t_of_row, s, NEG)], [vn_ref[0]])
        o = acc_ref[...] / l_ref[...]
        head_of_col = lax.broadcasted_iota(jnp.int32, (rows_q, ATT_DIM), 1) // HEAD_DIM
        head_of_row = lax.broadcasted_iota(jnp.int32, (rows_q, ATT_DIM), 0) % ATT_HEADS
        o = jnp.where(head_of_col == head_of_row, o, 0.0)
        y = jnp.sum(o.reshape(dec, ATT_HEADS, ATT_DIM), axis=1)
        o_ref[0] = ((y * _rms_scale(y)) * g_ref[...]).astype(BF16)


SAMPLE_PAGES_PER_STEP = 16


def _attn_sample(page_table, q_s, k_new, v_new, lf_new, g_att, cache_k, cache_v, cache_lft, pps):
    nb, n_pages = page_table.shape
    pps = pps if n_pages % pps == 0 else n_pages
    dec = q_s.shape[1]
    page = cache_k.shape[2]
    rows_q = dec * ATT_HEADS
    steps = n_pages // pps

    def paged(r):
        return lambda b, s, pt: (pt[b, s * pps + r], 0, 0)

    per_seq = lambda b, s, pt: (b, 0, 0)
    in_specs = [pl.BlockSpec((1, dec, ATT_DIM), per_seq),
                pl.BlockSpec((1, ATT_DIM, page), per_seq),
                pl.BlockSpec((1, ATT_DIM, page), per_seq),
                pl.BlockSpec((1, ATT_HEADS, page), per_seq),
                pl.BlockSpec((1, ATT_DIM), lambda b, s, pt: (0, 0))]
    in_specs += [pl.BlockSpec((1, ATT_DIM, page), paged(r)) for r in range(pps)]
    in_specs += [pl.BlockSpec((1, ATT_DIM, page), paged(r)) for r in range(pps)]
    in_specs += [pl.BlockSpec((1, ATT_HEADS, page), paged(r)) for r in range(pps)]
    vmem = 2 * 2 * pps * page * ATT_DIM * 4 + 4 * page * ATT_DIM * 4 + 16 * rows_q * ATT_DIM * 4 + (8 << 20)
    grid_spec = pltpu.PrefetchScalarGridSpec(
        num_scalar_prefetch=1,
        grid=(nb, steps),
        in_specs=in_specs,
        out_specs=pl.BlockSpec((1, dec, ATT_DIM), per_seq),
        scratch_shapes=[pltpu.VMEM((rows_q, ATT_DIM), F32),
                        pltpu.VMEM((rows_q, 1), F32),
                        pltpu.VMEM((rows_q, 1), F32),
                        pltpu.VMEM((rows_q, ATT_DIM), F32),
                        pltpu.VMEM((ATT_HEADS, 1), F32)])
    return pl.pallas_call(
        functools.partial(_sattn_kernel, pps=pps, dec=dec),
        grid_spec=grid_spec,
        out_shape=jax.ShapeDtypeStruct((nb, dec, ATT_DIM), BF16),
        compiler_params=_params(("arbitrary", "arbitrary"), vmem),
        name="attn_sample",
    )(page_table, q_s, k_new, v_new, lf_new, g_att,
      *([cache_k] * pps), *([cache_v] * pps), *([cache_lft] * pps))


def _out_kernel(x_ref, ma_ref, mc_ref, wa_ref, wc_ref, g_ref, x1_ref, h2_ref):
    x1 = x_ref[...] + (_dot(ma_ref[...], wa_ref[...]) + _dot(mc_ref[...], wc_ref[...]))
    x1_ref[...] = x1
    h2_ref[...] = ((x1 * _rms_scale(x1)) * g_ref[...]).astype(BF16)


def _out_proj(x, m_att, m_conv, w_a, w_c, g_mlp, tm):
    m, d = x.shape
    row = lambda i: (i, 0)
    const = lambda i: (0, 0)
    vmem = 2 * 2 * tm * d * 4 + 2 * tm * d * 2 + 4 * tm * ATT_DIM * 2 + 2 * 2 * ATT_DIM * d * 2 + (8 << 20)
    return pl.pallas_call(
        _out_kernel,
        grid=(m // tm,),
        in_specs=[pl.BlockSpec((tm, d), row),
                  pl.BlockSpec((tm, ATT_DIM), row),
                  pl.BlockSpec((tm, CONV_DIM), row),
                  pl.BlockSpec((ATT_DIM, d), const),
                  pl.BlockSpec((CONV_DIM, d), const),
                  pl.BlockSpec((1, d), const)],
        out_specs=[pl.BlockSpec((tm, d), row), pl.BlockSpec((tm, d), row)],
        out_shape=[jax.ShapeDtypeStruct((m, d), F32), jax.ShapeDtypeStruct((m, d), BF16)],
        compiler_params=_params(("arbitrary",), vmem),
        name="out_proj",
    )(x, m_att, m_conv, w_a, w_c, g_mlp)


def _mlp_kernel(h_ref, x1_ref, wu_ref, wd_ref, g_ref, y_ref, acc_ref):
    f = pl.program_id(1)

    @pl.when(f == 0)
    def _():
        acc_ref[...] = x1_ref[...]

    z = jnp.maximum(_dot(h_ref[...], wu_ref[...]), 0.0)
    acc_ref[...] += _dot((z * z).astype(BF16), wd_ref[...])

    @pl.when(f == pl.num_programs(1) - 1)
    def _():
        x2 = acc_ref[...]
        y_ref[...] = (x2 * _rms_scale(x2)) * g_ref[...]


def _mlp(h2, x1, w_up, w_down, g_final, tm, tf):
    m, d = x1.shape
    ff = w_up.shape[1]
    row = lambda i, f: (i, 0)
    vmem = (2 * tm * d * 2 + 2 * tm * d * 4 + 2 * 2 * d * tf * 2 + 2 * tm * d * 4 + tm * d * 4
            + 3 * tm * tf * 4 + (6 << 20))
    return pl.pallas_call(
        _mlp_kernel,
        grid=(m // tm, ff // tf),
        in_specs=[pl.BlockSpec((tm, d), row),
                  pl.BlockSpec((tm, d), row),
                  pl.BlockSpec((d, tf), lambda i, f: (0, f)),
                  pl.BlockSpec((tf, d), lambda i, f: (f, 0)),
                  pl.BlockSpec((1, d), lambda i, f: (0, 0))],
        out_specs=pl.BlockSpec((tm, d), row),
        out_shape=jax.ShapeDtypeStruct((m, d), F32),
        scratch_shapes=[pltpu.VMEM((tm, d), F32)],
        compiler_params=_params(("arbitrary", "arbitrary"), vmem),
        name="mlp",
    )(h2, x1, w_up, w_down, g_final)


def _keys_last(cache):
    n_pool, page = cache.shape[:2]
    return cache.transpose(0, 2, 3, 1).reshape(n_pool, ATT_DIM, page)


def _row_tile(m, want):
    return want if m % want == 0 else m


def _layer_weights(norm_mix, w_in, b_f, conv_w, norm_att_out, norm_conv_out, w_out, norm_mlp,
                   w_up, w_down):
    a = ATT_DIM
    c0 = 3 * a + ATT_HEADS
    w_qkv = w_in[:, :3 * a].astype(BF16)
    w_flt = w_in[:, 3 * a:c0].T.astype(BF16)
    w_conv = jnp.concatenate([w_in[:, c0:c0 + CONV_DIM],
                              w_in[:, c0 + 2 * CONV_DIM:c0 + 3 * CONV_DIM],
                              w_in[:, c0 + CONV_DIM:c0 + 2 * CONV_DIM]],
                             axis=1).astype(BF16)
    return dict(g_mix=norm_mix[None, :], w_qkv=w_qkv, w_conv=w_conv, w_flt=w_flt, b_f=b_f[:, None], conv_w=conv_w, g_att=norm_att_out[None, :],
                g_conv=norm_conv_out[None, :], w_a=w_out[:a].astype(BF16),
                w_c=w_out[a:].astype(BF16), g_mlp=norm_mlp[None, :],
                w_up=w_up.astype(BF16), w_down=w_down.astype(BF16))


def _tail(x, m_att, m_conv, w, g_final, tm, tf):
    x1, h2 = _out_proj(x, m_att, m_conv, w["w_a"], w["w_c"], w["g_mlp"], tm)
    return _mlp(h2, x1, w["w_up"], w["w_down"], g_final, tm, _row_tile(w["w_up"].shape[1], tf))


def kernel(x_prompt, x_sample, cache_k, cache_v, cache_logf, state_conv, page_table, norm_mix, w_in,
           b_f, conv_w, norm_att_out, norm_conv_out, w_out, norm_mlp, w_up, w_down, norm_final):
    depth = w_in.shape[0]
    assert depth == 1, "single-layer step"
    batch, seq, d = x_prompt.shape
    nb, dec, _ = x_sample.shape
    page = cache_k.shape[2]
    w = _layer_weights(norm_mix[0], w_in[0], b_f[0], conv_w[0], norm_att_out[0], norm_conv_out[0],
                       w_out[0], norm_mlp[0], w_up[0], w_down[0])
    g_final = norm_final[None, :]

    xp = x_prompt.reshape(batch * seq, d)
    pairs_p, k_p, v_p, lft_p, m_conv, tail_p = _in_proj(xp, w, _row_tile(seq, 256), seq)
    tq = _row_tile(seq, 512)
    m_att = _attn_prompt(pairs_p, lft_p, w["g_att"], batch, seq, tq, _row_tile(seq, 256))
    y_p = _tail(xp, m_att, m_conv, w, g_final, _row_tile(seq, 512), 1024)

    ms = nb * dec
    xs = x_sample.reshape(ms, d)
    st = state_conv[0]
    zeros = jnp.zeros((nb, dec - 1, CONV_DIM), F32)
    f1 = jnp.concatenate([st[:, 1:2], zeros], axis=1).reshape(ms, CONV_DIM)
    f2 = jnp.concatenate([st, zeros[:, 1:]], axis=1).reshape(ms, CONV_DIM)
    pairs_s, k_s, v_s, lft_s, m_conv_s, u_s = _in_proj(xs, w, ms, dec, fills=(f1, f2))
    q_s = pairs_s[0].transpose(1, 0, 2).reshape(nb, dec, ATT_DIM)
    pad = ((0, 0), (0, 0), (0, page - dec))
    k_new = jnp.pad(k_s.reshape(nb, dec, ATT_DIM).transpose(0, 2, 1), pad)
    v_new = jnp.pad(v_s.reshape(nb, dec, ATT_DIM).transpose(0, 2, 1), pad)
    lf_new = jnp.pad(lft_s.reshape(ATT_HEADS, nb, dec).transpose(1, 0, 2), pad)
    n_pool = cache_k.shape[1]
    m_att_s = _attn_sample(page_table, q_s, k_new, v_new, lf_new, w["g_att"],
                           _keys_last(cache_k[0]), _keys_last(cache_v[0]),
                           cache_logf[0].transpose(0, 2, 1), SAMPLE_PAGES_PER_STEP)
    y_s = _tail(xs, m_att_s.reshape(ms, ATT_DIM), m_conv_s, w, g_final, ms, 1024)

    hd = (ATT_HEADS, HEAD_DIM)
    return (y_p.reshape(batch, seq, d),
            y_s.reshape(nb, dec, d),
            k_p.reshape(1, batch, seq, *hd),
            v_p.reshape(1, batch, seq, *hd),
            lft_p.T.reshape(1, batch, seq, ATT_HEADS),
            tail_p[:, PAD_ROWS - (CONV_W - 1):][None],
            k_s.reshape(1, nb, dec, *hd),
            v_s.reshape(1, nb, dec, *hd),
            lft_s.T.reshape(1, nb, dec, ATT_HEADS),
            u_s.reshape(nb, dec, CONV_DIM)[:, dec - (CONV_W - 1):][None])
```

```python
import functools

import jax
import jax.numpy as jnp
from jax import lax
from jax.experimental import pallas as pl
from jax.experimental.pallas import tpu as pltpu

F32 = jnp.float32
BF16 = jnp.bfloat16

ATT_HEADS = 16
HEAD_DIM = 64
ATT_DIM = ATT_HEADS * HEAD_DIM
CONV_DIM = 1024
CONV_W = 3
EPS = 1e-6
LANES = 128
HEAD_PAIRS = ATT_DIM // LANES
LOG2E = 1.4426950408889634
Q_SCALE = HEAD_DIM ** -0.5 * LOG2E
NEG = -1e30
V7X_VMEM_LIMIT = 56 * 1024 * 1024


def _params(semantics, vmem_bytes):
    return pltpu.CompilerParams(dimension_semantics=semantics,
                                vmem_limit_bytes=min(int(vmem_bytes), V7X_VMEM_LIMIT))


def _dot(a, b):
    return jnp.dot(a, b, preferred_element_type=F32)


def _dot_nt(a, b):
    return lax.dot_general(a, b, (((1,), (1,)), ((), ())), preferred_element_type=F32)


def _split_bf16(x):
    hi = x.astype(BF16)
    lo = (x - hi.astype(F32)).astype(BF16)
    return hi, lo


def _upper_ones(n):
    r = lax.broadcasted_iota(jnp.int32, (n, n), 0)
    c = lax.broadcasted_iota(jnp.int32, (n, n), 1)
    return (r <= c).astype(BF16)


def _prefix_lanes(x, upper):
    hi, lo = _split_bf16(x)
    return _dot(hi, upper) + _dot(lo, upper)


def _rms_scale(x):
    return lax.rsqrt(jnp.mean(x * x, axis=-1, keepdims=True) + EPS)


def _log_sigmoid(z):
    return jnp.minimum(z, 0.0) - jnp.log1p(jnp.exp(-jnp.abs(z)))


PAD_ROWS = 8


def _resident(shape):
    return pl.BlockSpec(shape, lambda i: (0,) * len(shape), pipeline_mode=pl.Buffered(1))


def _in_proj_kernel(*refs, tm, blocks_per_seq, period):
    if period is None:
        (x0_ref, xn_ref, g_ref, wa_ref, wc_ref, wflt_ref, bf_ref, cw_ref, gout_ref,
         pairs_ref, k_ref, v_ref, lft_ref, mconv_ref, tail_ref, h_ref, ubuf_ref) = refs
    else:
        (x0_ref, xn_ref, g_ref, wa_ref, wc_ref, wflt_ref, bf_ref, cw_ref, gout_ref, f1_ref, f2_ref,
         pairs_ref, k_ref, v_ref, lft_ref, mconv_ref, tail_ref, h_ref, ubuf_ref) = refs
    i = pl.program_id(0)

    def normed(x):
        return ((x * _rms_scale(x)) * g_ref[...]).astype(BF16)

    @pl.when(i == 0)
    def _():
        h_ref[0] = normed(x0_ref[...])

    @pl.when(i % blocks_per_seq == 0)
    def _():
        ubuf_ref[0:PAD_ROWS, :] = jnp.zeros((PAD_ROWS, CONV_DIM), F32)

    slot = i % 2
    hb = h_ref[slot]
    h_ref[1 - slot] = normed(xn_ref[...])

    def proj(j):
        w_ref, c = (wa_ref, j) if j < 3 else (wc_ref, j - 3)
        return _dot(hb, w_ref[:, c * ATT_DIM:(c + 1) * ATT_DIM])

    lft_ref[...] = _log_sigmoid(_dot_nt(wflt_ref[...], hb) + bf_ref[...])
    u = proj(4) * proj(3)
    ubuf_ref[PAD_ROWS:PAD_ROWS + tm, :] = u
    u1 = ubuf_ref[PAD_ROWS - 1:PAD_ROWS - 1 + tm, :]
    u2 = ubuf_ref[PAD_ROWS - 2:PAD_ROWS - 2 + tm, :]
    if period is not None:
        pos = lax.broadcasted_iota(jnp.int32, (tm, CONV_DIM), 0) % period
        u1 = jnp.where(pos < 1, f1_ref[...], u1)
        u2 = jnp.where(pos < 2, f2_ref[...], u2)
    conv = u2 * cw_ref[0:1, :] + u1 * cw_ref[1:2, :] + u * cw_ref[2:3, :]
    last = ubuf_ref[tm:tm + PAD_ROWS, :]
    ubuf_ref[0:PAD_ROWS, :] = last
    if period is None:
        tail_ref[0] = last
    else:
        tail_ref[...] = u
    y = proj(5) * conv
    mconv_ref[...] = ((y * _rms_scale(y)) * gout_ref[...]).astype(BF16)
    for j, heads_ref in enumerate((None, k_ref, v_ref)):
        r = proj(j)
        scaled = r * Q_SCALE if j == 0 else r
        for hp in range(HEAD_PAIRS):
            pairs_ref[j, hp] = scaled[:, hp * LANES:(hp + 1) * LANES].astype(BF16)
        if heads_ref is not None:
            for h in range(ATT_HEADS):
                heads_ref[pl.ds(h, tm, stride=ATT_HEADS), :] = r[:, h * HEAD_DIM:(h + 1) * HEAD_DIM]


def _in_proj(x, w, tm, seq, fills=None):
    m, d = x.shape
    n = m // tm
    row = lambda i: (i, 0)
    in_specs = [pl.BlockSpec((tm, d), lambda i: (0, 0), pipeline_mode=pl.Buffered(1)),
                pl.BlockSpec((tm, d), lambda i: (jnp.minimum(i + 1, n - 1), 0)),
                _resident((1, d)),
                _resident((d, 3 * ATT_DIM)),
                _resident((d, 3 * CONV_DIM)),
                _resident((ATT_HEADS, d)),
                _resident((ATT_HEADS, 1)),
                _resident((CONV_W, CONV_DIM)),
                _resident((1, CONV_DIM))]
    args = [x, x, w["g_mix"], w["w_qkv"], w["w_conv"], w["w_flt"], w["b_f"], w["conv_w"],
            w["g_conv"]]
    if fills is None:
        blocks_per_seq = seq // tm
        tail_spec = pl.BlockSpec((1, PAD_ROWS, CONV_DIM), lambda i: (i // blocks_per_seq, 0, 0))
        tail_shape = jax.ShapeDtypeStruct((m // seq, PAD_ROWS, CONV_DIM), F32)
        period = None
    else:
        blocks_per_seq = 1
        in_specs += [pl.BlockSpec((tm, CONV_DIM), row), pl.BlockSpec((tm, CONV_DIM), row)]
        args += list(fills)
        tail_spec = pl.BlockSpec((tm, CONV_DIM), row)
        tail_shape = jax.ShapeDtypeStruct((m, CONV_DIM), F32)
        period = seq
    heads_rows = tm * ATT_HEADS * LANES * 4
    vmem = (3 * tm * d * 4 + d * 6 * ATT_DIM * 2 + 2 * 3 * tm * ATT_DIM * 2 + 2 * 2 * heads_rows
            + 2 * tm * d * 2 + 12 * tm * ATT_DIM * 4 + (6 << 20))
    heads_spec = pl.BlockSpec((tm * ATT_HEADS, HEAD_DIM), row)
    heads_shape = jax.ShapeDtypeStruct((m * ATT_HEADS, HEAD_DIM), F32)
    return pl.pallas_call(
        functools.partial(_in_proj_kernel, tm=tm, blocks_per_seq=blocks_per_seq, period=period),
        grid=(n,),
        in_specs=in_specs,
        out_specs=[pl.BlockSpec((3, HEAD_PAIRS, tm, LANES), lambda i: (0, 0, i, 0)),
                   heads_spec, heads_spec,
                   pl.BlockSpec((ATT_HEADS, tm), lambda i: (0, i)),
                   pl.BlockSpec((tm, CONV_DIM), row),
                   tail_spec],
        out_shape=[jax.ShapeDtypeStruct((3, HEAD_PAIRS, m, LANES), BF16),
                   heads_shape, heads_shape,
                   jax.ShapeDtypeStruct((ATT_HEADS, m), F32),
                   jax.ShapeDtypeStruct((m, CONV_DIM), BF16),
                   tail_shape],
        scratch_shapes=[pltpu.VMEM((2, tm, d), BF16),
                        pltpu.VMEM((tm + PAD_ROWS, CONV_DIM), F32)],
        compiler_params=_params(("arbitrary",), vmem),
        name="in_proj",
    )(*args)


def _attn_kernel(q_ref, k_ref, v_ref, lft_ref, g_ref, o_ref,
                 negc_ref, qx_ref, vx_ref, m_ref, acc_ref, *, tq, tk, seq):
    qi = pl.program_id(1)
    n_kv = seq // tk
    diag = tq // tk
    n_chunk = tk // LANES

    @pl.when(qi == 0)
    def _():
        upper = _upper_ones(tk)
        carry = jnp.zeros((ATT_HEADS, 1), F32)
        for kb in range(n_kv):
            c_blk = _prefix_lanes(lft_ref[:, kb * tk:(kb + 1) * tk], upper) + carry
            negc_ref[kb * ATT_HEADS:(kb + 1) * ATT_HEADS, :] = c_blk * -LOG2E
            carry = c_blk[:, tk - 1:tk]

        lane_k = lax.broadcasted_iota(jnp.int32, (tk, LANES), 1)

        def indicator(mask):
            return jnp.where(mask, 1.0, 0.0).astype(BF16)

        keep = (indicator(lane_k < HEAD_DIM), indicator(lane_k >= HEAD_DIM))

        def build(kb, _):
            rows_k = pl.ds(pl.multiple_of(kb * tk, tk), tk)
            for hp in range(HEAD_PAIRS):
                v2 = v_ref[hp, rows_k, :]
                for a in range(2):
                    vx_ref[2 * hp + a, rows_k, :] = v2 * keep[a] + keep[1 - a]
            return 0

        lax.fori_loop(0, n_kv, build, 0)

    is_a = lax.broadcasted_iota(jnp.int32, (tq, LANES), 1) < HEAD_DIM
    for hp in range(HEAD_PAIRS):
        q2 = q_ref[hp]
        zero = jnp.zeros_like(q2)
        qx_ref[hp, 0:tq, :] = jnp.where(is_a, q2, zero)
        qx_ref[hp, tq:2 * tq, :] = jnp.where(is_a, zero, q2)
    m_ref[...] = jnp.full(m_ref.shape, NEG, F32)
    acc_ref[...] = jnp.zeros(acc_ref.shape, F32)

    def kv_step(kv, n_blk, masked, row_lo=0):
        rows_k = pl.ds(pl.multiple_of(kv * tk, tk), n_blk * tk)
        col_shift = kv * tk - qi * tq
        n_row = tq - row_lo
        rows_q = slice(row_lo, tq)
        if masked:
            row = lax.broadcasted_iota(jnp.int32, (n_row, LANES), 0) + row_lo
            col = lax.broadcasted_iota(jnp.int32, (n_row, LANES), 1)
        for hp in range(HEAD_PAIRS):
            if row_lo:
                lhs = jnp.concatenate([qx_ref[hp, rows_q, :], qx_ref[hp, tq + row_lo:2 * tq, :]], axis=0)
            else:
                lhs = qx_ref[hp]
            s2 = _dot_nt(lhs, k_ref[hp, rows_k, :])
            for a in range(2):
                h = 2 * hp + a
                chunks = []
                for blk in range(n_blk):
                    bias = negc_ref[pl.ds((kv + blk) * ATT_HEADS + h, 1), :]
                    for c in range(n_chunk):
                        off = blk * tk + c * LANES
                        sc = (s2[a * n_row:(a + 1) * n_row, off:off + LANES]
                              + bias[:, c * LANES:(c + 1) * LANES])
                        if masked:
                            sc = jnp.where(col + (col_shift + off) <= row, sc, NEG)
                        chunks.append(sc)
                mx = chunks[0]
                for sc in chunks[1:]:
                    mx = jnp.maximum(mx, sc)
                m_old = m_ref[h, rows_q, :]
                m_new = jnp.maximum(m_old, jnp.max(mx, axis=-1, keepdims=True))
                alpha = jnp.exp2(m_old - m_new)
                p = jnp.concatenate([jnp.exp2(sc - m_new) for sc in chunks], axis=1).astype(BF16)
                acc_ref[h, rows_q, :] = alpha * acc_ref[h, rows_q, :] + _dot(p, vx_ref[h, rows_k, :])
                m_ref[h, rows_q, :] = m_new

    def pair_step(j, _):
        kv_step(2 * j, 2, False)
        return 0

    def full_step(kv, _):
        kv_step(kv, 1, False)
        return 0

    n_full = qi * diag
    n_pair = lax.shift_right_logical(n_full, 1)
    lax.fori_loop(0, n_pair, pair_step, 0)
    lax.fori_loop(2 * n_pair, n_full, full_step, 0)
    for d in range(diag):
        def diag_step(kv, _, row_lo=d * tk):
            kv_step(kv, 1, True, row_lo)
            return 0

        lax.fori_loop(n_full + d, n_full + d + 1, diag_step, 0)

    outs = []
    ss = jnp.zeros((tq, LANES), F32)
    for hp in range(HEAD_PAIRS):
        acc_a, acc_b = acc_ref[2 * hp], acc_ref[2 * hp + 1]
        den = pltpu.roll(jnp.where(is_a, acc_b, acc_a), HEAD_DIM, axis=1)
        o = jnp.where(is_a, acc_a, acc_b) / den
        outs.append(o)
        ss = ss + o * o
    inv = lax.rsqrt(jnp.sum(ss, axis=-1, keepdims=True) * (1.0 / ATT_DIM) + EPS)
    for hp in range(HEAD_PAIRS):
        sl = slice(hp * LANES, (hp + 1) * LANES)
        o_ref[:, sl] = ((outs[hp] * inv) * g_ref[:, sl]).astype(BF16)


def _attn_prompt(qkv_pairs, lft, g_att, batch, seq, tq, tk):
    m = batch * seq
    nq = seq // tq
    vmem = (2 * 2 * seq * ATT_DIM * 2 + seq * ATT_DIM * 2 * 2 + 2 * ATT_HEADS * tq * LANES * 4
            + 4 * tq * ATT_DIM * 2 + (12 << 20))
    return pl.pallas_call(
        functools.partial(_attn_kernel, tq=tq, tk=tk, seq=seq),
        grid=(batch, nq),
        in_specs=[pl.BlockSpec((None, HEAD_PAIRS, tq, LANES), lambda b, i: (0, 0, b * nq + i, 0)),
                  pl.BlockSpec((None, HEAD_PAIRS, seq, LANES), lambda b, i: (1, 0, b, 0)),
                  pl.BlockSpec((None, HEAD_PAIRS, seq, LANES), lambda b, i: (2, 0, b, 0)),
                  pl.BlockSpec((ATT_HEADS, seq), lambda b, i: (0, b)),
                  pl.BlockSpec((1, ATT_DIM), lambda b, i: (0, 0))],
        out_specs=pl.BlockSpec((tq, ATT_DIM), lambda b, i: (b * nq + i, 0)),
        out_shape=jax.ShapeDtypeStruct((m, ATT_DIM), BF16),
        scratch_shapes=[pltpu.VMEM((seq // tk * ATT_HEADS, tk), F32),
                        pltpu.VMEM((HEAD_PAIRS, 2 * tq, LANES), BF16),
                        pltpu.VMEM((ATT_HEADS, seq, LANES), BF16),
                        pltpu.VMEM((ATT_HEADS, tq, LANES), F32),
                        pltpu.VMEM((ATT_HEADS, tq, LANES), F32)],
        compiler_params=_params(("arbitrary", "arbitrary"), vmem),
        name="attn_prompt",
    )(qkv_pairs, qkv_pairs, qkv_pairs, lft, g_att)


def _sattn_kernel(pt_ref, q_ref, kn_ref, vn_ref, lfn_ref, g_ref, *refs, pps, dec):
    del pt_ref
    k_refs, v_refs, lf_refs = refs[:pps], refs[pps:2 * pps], refs[2 * pps:3 * pps]
    o_ref, qbd_ref, m_ref, l_ref, acc_ref, carry_ref = refs[3 * pps:]
    step = pl.program_id(1)
    rows_q = dec * ATT_HEADS
    page = k_refs[0].shape[2]

    @pl.when(step == 0)
    def _():
        q = q_ref[0].astype(F32)
        head_of_col = lax.broadcasted_iota(jnp.int32, (ATT_HEADS, ATT_DIM), 1) // HEAD_DIM
        head_of_row = lax.broadcasted_iota(jnp.int32, (ATT_HEADS, ATT_DIM), 0)
        own = head_of_col == head_of_row
        for t in range(dec):
            qt = jnp.broadcast_to(q[t:t + 1, :], (ATT_HEADS, ATT_DIM))
            qbd_ref[t * ATT_HEADS:(t + 1) * ATT_HEADS, :] = jnp.where(own, qt, 0.0)
        m_ref[...] = jnp.full(m_ref.shape, NEG, F32)
        l_ref[...] = jnp.zeros(l_ref.shape, F32)
        acc_ref[...] = jnp.zeros(acc_ref.shape, F32)
        carry_ref[...] = jnp.zeros(carry_ref.shape, F32)

    upper = _upper_ones(page)

    def update(scores, values):
        m_old = m_ref[...]
        m_new = m_old
        for s in scores:
            m_new = jnp.maximum(m_new, jnp.max(s, axis=-1, keepdims=True))
        alpha = jnp.exp2(m_old - m_new)
        l_new = alpha * l_ref[...]
        pv = None
        for s, v in zip(scores, values):
            p = jnp.exp2(s - m_new)
            l_new = l_new + jnp.sum(p, axis=-1, keepdims=True)
            d = _dot_nt(p, v)
            pv = d if pv is None else pv + d
        acc_ref[...] = alpha * acc_ref[...] + pv
        l_ref[...] = l_new
        m_ref[...] = m_new

    def scores_of(k_page, c_page):
        return _dot(qbd_ref[...], k_page) - jnp.tile(c_page * LOG2E, (dec, 1))

    lf_all = jnp.concatenate([lf_refs[r][0] for r in range(pps)], axis=0)
    within = _prefix_lanes(lf_all, upper)
    car = carry_ref[...]
    scores = []
    for r in range(pps):
        blk = within[r * ATT_HEADS:(r + 1) * ATT_HEADS, :]
        scores.append(scores_of(k_refs[r][0], blk + car))
        car = car + blk[:, page - 1:page]
    carry_ref[...] = car
    update(scores, [v_refs[r][0] for r in range(pps)])

    @pl.when(step == pl.num_programs(1) - 1)
    def _():
        c_new = _prefix_lanes(lfn_ref[0], upper) + carry_ref[...]
        s = scores_of(kn_ref[0], c_new)
        t_of_row = lax.broadcasted_iota(jnp.int32, (rows_q, page), 0) // ATT_HEADS
        key = lax.broadcasted_iota(jnp.int32, (rows_q, page), 1)
        update([jnp.where(key <= t_of_row, s, NEG)], [vn_ref[0]])
        o = acc_ref[...] / l_ref[...]
        head_of_col = lax.broadcasted_iota(jnp.int32, (rows_q, ATT_DIM), 1) // HEAD_DIM
        head_of_row = lax.broadcasted_iota(jnp.int32, (rows_q, ATT_DIM), 0) % ATT_HEADS
        o = jnp.where(head_of_col == head_of_row, o, 0.0)
        y = jnp.sum(o.reshape(dec, ATT_HEADS, ATT_DIM), axis=1)
        o_ref[0] = ((y * _rms_scale(y)) * g_ref[...]).astype(BF16)


SAMPLE_PAGES_PER_STEP = 16


def _attn_sample(page_table, q_s, k_new, v_new, lf_new, g_att, cache_k, cache_v, cache_lft, pps):
    nb, n_pages = page_table.shape
    pps = pps if n_pages % pps == 0 else n_pages
    dec = q_s.shape[1]
    page = cache_k.shape[2]
    rows_q = dec * ATT_HEADS
    steps = n_pages // pps

    def paged(r):
        return lambda b, s, pt: (pt[b, s * pps + r], 0, 0)

    per_seq = lambda b, s, pt: (b, 0, 0)
    in_specs = [pl.BlockSpec((1, dec, ATT_DIM), per_seq),
                pl.BlockSpec((1, ATT_DIM, page), per_seq),
                pl.BlockSpec((1, ATT_DIM, page), per_seq),
                pl.BlockSpec((1, ATT_HEADS, page), per_seq),
                pl.BlockSpec((1, ATT_DIM), lambda b, s, pt: (0, 0))]
    in_specs += [pl.BlockSpec((1, ATT_DIM, page), paged(r)) for r in range(pps)]
    in_specs += [pl.BlockSpec((1, ATT_DIM, page), paged(r)) for r in range(pps)]
    in_specs += [pl.BlockSpec((1, ATT_HEADS, page), paged(r)) for r in range(pps)]
    vmem = 2 * 2 * pps * page * ATT_DIM * 4 + 4 * page * ATT_DIM * 4 + 16 * rows_q * ATT_DIM * 4 + (8 << 20)
    grid_spec = pltpu.PrefetchScalarGridSpec(
        num_scalar_prefetch=1,
        grid=(nb, steps),
        in_specs=in_specs,
        out_specs=pl.BlockSpec((1, dec, ATT_DIM), per_seq),
        scratch_shapes=[pltpu.VMEM((rows_q, ATT_DIM), F32),
                        pltpu.VMEM((rows_q, 1), F32),
                        pltpu.VMEM((rows_q, 1), F32),
                        pltpu.VMEM((rows_q, ATT_DIM), F32),
                        pltpu.VMEM((ATT_HEADS, 1), F32)])
    return pl.pallas_call(
        functools.partial(_sattn_kernel, pps=pps, dec=dec),
        grid_spec=grid_spec,
        out_shape=jax.ShapeDtypeStruct((nb, dec, ATT_DIM), BF16),
        compiler_params=_params(("arbitrary", "arbitrary"), vmem),
        name="attn_sample",
    )(page_table, q_s, k_new, v_new, lf_new, g_att,
      *([cache_k] * pps), *([cache_v] * pps), *([cache_lft] * pps))


def _out_kernel(x_ref, ma_ref, mc_ref, wa_ref, wc_ref, g_ref, x1_ref, h2_ref):
    x1 = x_ref[...] + (_dot(ma_ref[...], wa_ref[...]) + _dot(mc_ref[...], wc_ref[...]))
    x1_ref[...] = x1
    h2_ref[...] = ((x1 * _rms_scale(x1)) * g_ref[...]).astype(BF16)


def _out_proj(x, m_att, m_conv, w_a, w_c, g_mlp, tm):
    m, d = x.shape
    row = lambda i: (i, 0)
    const = lambda i: (0, 0)
    vmem = 2 * 2 * tm * d * 4 + 2 * tm * d * 2 + 4 * tm * ATT_DIM * 2 + 2 * 2 * ATT_DIM * d * 2 + (8 << 20)
    return pl.pallas_call(
        _out_kernel,
        grid=(m // tm,),
        in_specs=[pl.BlockSpec((tm, d), row),
                  pl.BlockSpec((tm, ATT_DIM), row),
                  pl.BlockSpec((tm, CONV_DIM), row),
                  pl.BlockSpec((ATT_DIM, d), const),
                  pl.BlockSpec((CONV_DIM, d), const),
                  pl.BlockSpec((1, d), const)],
        out_specs=[pl.BlockSpec((tm, d), row), pl.BlockSpec((tm, d), row)],
        out_shape=[jax.ShapeDtypeStruct((m, d), F32), jax.ShapeDtypeStruct((m, d), BF16)],
        compiler_params=_params(("arbitrary",), vmem),
        name="out_proj",
    )(x, m_att, m_conv, w_a, w_c, g_mlp)


def _mlp_kernel(h_ref, x1_ref, wu_ref, wd_ref, g_ref, y_ref, acc_ref):
    f = pl.program_id(1)

    @pl.when(f == 0)
    def _():
        acc_ref[...] = x1_ref[...]

    z = jnp.maximum(_dot(h_ref[...], wu_ref[...]), 0.0)
    acc_ref[...] += _dot((z * z).astype(BF16), wd_ref[...])

    @pl.when(f == pl.num_programs(1) - 1)
    def _():
        x2 = acc_ref[...]
        y_ref[...] = (x2 * _rms_scale(x2)) * g_ref[...]


def _mlp(h2, x1, w_up, w_down, g_final, tm, tf):
    m, d = x1.shape
    ff = w_up.shape[1]
    row = lambda i, f: (i, 0)
    vmem = (2 * tm * d * 2 + 2 * tm * d * 4 + 2 * 2 * d * tf * 2 + 2 * tm * d * 4 + tm * d * 4
            + 3 * tm * tf * 4 + (6 << 20))
    return pl.pallas_call(
        _mlp_kernel,
        grid=(m // tm, ff // tf),
        in_specs=[pl.BlockSpec((tm, d), row),
                  pl.BlockSpec((tm, d), row),
                  pl.BlockSpec((d, tf), lambda i, f: (0, f)),
                  pl.BlockSpec((tf, d), lambda i, f: (f, 0)),
                  pl.BlockSpec((1, d), lambda i, f: (0, 0))],
        out_specs=pl.BlockSpec((tm, d), row),
        out_shape=jax.ShapeDtypeStruct((m, d), F32),
        scratch_shapes=[pltpu.VMEM((tm, d), F32)],
        compiler_params=_params(("arbitrary", "arbitrary"), vmem),
        name="mlp",
    )(h2, x1, w_up, w_down, g_final)


def _keys_last(cache):
    n_pool, page = cache.shape[:2]
    return cache.transpose(0, 2, 3, 1).reshape(n_pool, ATT_DIM, page)


def _row_tile(m, want):
    return want if m % want == 0 else m


def _layer_weights(norm_mix, w_in, b_f, conv_w, norm_att_out, norm_conv_out, w_out, norm_mlp,
                   w_up, w_down):
    a = ATT_DIM
    c0 = 3 * a + ATT_HEADS
    w_qkv = w_in[:, :3 * a].astype(BF16)
    w_flt = w_in[:, 3 * a:c0].T.astype(BF16)
    w_conv = jnp.concatenate([w_in[:, c0:c0 + CONV_DIM],
                              w_in[:, c0 + 2 * CONV_DIM:c0 + 3 * CONV_DIM],
                              w_in[:, c0 + CONV_DIM:c0 + 2 * CONV_DIM]],
                             axis=1).astype(BF16)
    return dict(g_mix=norm_mix[None, :], w_qkv=w_qkv, w_conv=w_conv, w_flt=w_flt, b_f=b_f[:, None], conv_w=conv_w, g_att=norm_att_out[None, :],
                g_conv=norm_conv_out[None, :], w_a=w_out[:a].astype(BF16),
                w_c=w_out[a:].astype(BF16), g_mlp=norm_mlp[None, :],
                w_up=w_up.astype(BF16), w_down=w_down.astype(BF16))


def _tail(x, m_att, m_conv, w, g_final, tm, tf):
    x1, h2 = _out_proj(x, m_att, m_conv, w["w_a"], w["w_c"], w["g_mlp"], tm)
    return _mlp(h2, x1, w["w_up"], w["w_down"], g_final, tm, _row_tile(w["w_up"].shape[1], tf))


def kernel(x_prompt, x_sample, cache_k, cache_v, cache_logf, state_conv, page_table, norm_mix, w_in,
           b_f, conv_w, norm_att_out, norm_conv_out, w_out, norm_mlp, w_up, w_down, norm_final):
    depth = w_in.shape[0]
    assert depth == 1, "single-layer step"
    batch, seq, d = x_prompt.shape
    nb, dec, _ = x_sample.shape
    page = cache_k.shape[2]
    w = _layer_weights(norm_mix[0], w_in[0], b_f[0], conv_w[0], norm_att_out[0], norm_conv_out[0],
                       w_out[0], norm_mlp[0], w_up[0], w_down[0])
    g_final = norm_final[None, :]

    xp = x_prompt.reshape(batch * seq, d)
    pairs_p, k_p, v_p, lft_p, m_conv, tail_p = _in_proj(xp, w, _row_tile(seq, 256), seq)
    tq = _row_tile(seq, 512)
    m_att = _attn_prompt(pairs_p, lft_p, w["g_att"], batch, seq, tq, _row_tile(seq, 256))
    y_p = _tail(xp, m_att, m_conv, w, g_final, _row_tile(seq, 512), 1024)

    ms = nb * dec
    xs = x_sample.reshape(ms, d)
    st = state_conv[0]
    zeros = jnp.zeros((nb, dec - 1, CONV_DIM), F32)
    f1 = jnp.concatenate([st[:, 1:2], zeros], axis=1).reshape(ms, CONV_DIM)
    f2 = jnp.concatenate([st, zeros[:, 1:]], axis=1).reshape(ms, CONV_DIM)
    pairs_s, k_s, v_s, lft_s, m_conv_s, u_s = _in_proj(xs, w, ms, dec, fills=(f1, f2))
    q_s = pairs_s[0].transpose(1, 0, 2).reshape(nb, dec, ATT_DIM)
    pad = ((0, 0), (0, 0), (0, page - dec))
    k_new = jnp.pad(k_s.reshape(nb, dec, ATT_DIM).transpose(0, 2, 1), pad)
    v_new = jnp.pad(v_s.reshape(nb, dec, ATT_DIM).transpose(0, 2, 1), pad)
    lf_new = jnp.pad(lft_s.reshape(ATT_HEADS, nb, dec).transpose(1, 0, 2), pad)
    n_pool = cache_k.shape[1]
    m_att_s = _attn_sample(page_table, q_s, k_new, v_new, lf_new, w["g_att"],
                           _keys_last(cache_k[0]), _keys_last(cache_v[0]),
                           cache_logf[0].transpose(0, 2, 1), SAMPLE_PAGES_PER_STEP)
    y_s = _tail(xs, m_att_s.reshape(ms, ATT_DIM), m_conv_s, w, g_final, ms, 1024)

    hd = (ATT_HEADS, HEAD_DIM)
    return (y_p.reshape(batch, seq, d),
            y_s.reshape(nb, dec, d),
            k_p.reshape(1, batch, seq, *hd),
            v_p.reshape(1, batch, seq, *hd),
            lft_p.T.reshape(1, batch, seq, ATT_HEADS),
            tail_p[:, PAD_ROWS - (CONV_W - 1):][None],
            k_s.reshape(1, nb, dec, *hd),
            v_s.reshape(1, nb, dec, *hd),
            lft_s.T.reshape(1, nb, dec, ATT_HEADS),
            u_s.reshape(nb, dec, CONV_DIM)[:, dec - (CONV_W - 1):][None])
```
